```python
import jax, jax.numpy as jnp
from jax import lax
import numpy as np

D_MODEL = 1024
BATCH = 2
SEQ = 16384
DEPTH = 4

RET_HEADS = 4
RET_QK_DIM = 128
RET_V_DIM = 256
MLSTM_HEADS = 4
MLSTM_QK_DIM = 128
MLSTM_V_DIM = 256
CONV_WIDTH = 4
CHUNK = 128
D_FF = -(-8 * D_MODEL // (3 * 256)) * 256
ROPE_BASE = 10000.0
NORM_EPS = 1e-5
DEEPNORM_ALPHA = (2 * DEPTH) ** 0.25
DEEPNORM_BETA = (8 * DEPTH) ** -0.25

RET_QK = RET_HEADS * RET_QK_DIM
RET_V = RET_HEADS * RET_V_DIM
ML_QK = MLSTM_HEADS * MLSTM_QK_DIM
ML_V = MLSTM_HEADS * MLSTM_V_DIM
IN_SIZES = (RET_QK, RET_QK, RET_V, RET_V, 2 * ML_QK, ML_V, ML_V, MLSTM_HEADS, MLSTM_HEADS, D_MODEL, D_MODEL)
IN_BETA = (1.0, 1.0, DEEPNORM_BETA, 1.0, 1.0, DEEPNORM_BETA, 1.0, 1.0, 1.0, 1.0, 1.0)
IN_OFFSETS = tuple(int(o) for o in np.cumsum(IN_SIZES)[:-1])
D_IN = int(sum(IN_SIZES))

kernel_name = "retnet_mlstm_gated_hybrid_deepnorm"


def _layer_norm(x, g, b):
    xf = x.astype(jnp.float32)
    mu = xf.mean(-1, keepdims=True)
    var = jnp.square(xf - mu).mean(-1, keepdims=True)
    return ((xf - mu) * lax.rsqrt(var + NORM_EPS) * g.astype(jnp.float32) + b.astype(jnp.float32)).astype(x.dtype)


def _head_norm(h, g):
    B, S, H, d = h.shape
    mu = h.mean(-1, keepdims=True)
    var = jnp.square(h - mu).mean(-1, keepdims=True)
    hn = ((h - mu) * lax.rsqrt(var + NORM_EPS)).reshape(B, S, H * d)
    return hn * g.astype(jnp.float32)


def _rotary(x, pos):
    half = x.shape[-1] // 2
    inv_freq = ROPE_BASE ** (-jnp.arange(half, dtype=jnp.float32) / half)
    ang = pos.astype(jnp.float32)[:, None] * inv_freq[None, :]
    cos = jnp.cos(ang)[None, :, None, :]
    sin = jnp.sin(ang)[None, :, None, :]
    x1, x2 = x[..., :half], x[..., half:]
    return jnp.concatenate([x1 * cos - x2 * sin, x1 * sin + x2 * cos], axis=-1)


def _causal_depthwise_conv(x, w, b):
    C = x.shape[-1]
    y = lax.conv_general_dilated(
        x, w[:, None, :].astype(x.dtype), window_strides=(1,), padding=[(CONV_WIDTH - 1, 0)],
        dimension_numbers=('NWC', 'WIO', 'NWC'), feature_group_count=C)
    return y + b.astype(x.dtype)


def _retention(q, k, v):
    B, S, H, dk = q.shape
    dv = v.shape[-1]
    L = CHUNK
    N = S // L
    log_gamma = jnp.log(1.0 - jnp.power(2.0, -5.0 - jnp.arange(H, dtype=jnp.float32)))
    q = q.reshape(B, N, L, H, dk)
    k = k.reshape(B, N, L, H, dk)
    v = v.reshape(B, N, L, H, dv)
    idx = jnp.arange(L, dtype=jnp.float32)
    rel = idx[:, None] - idx[None, :]
    causal = rel >= 0
    decay = jnp.where(causal[None], jnp.exp(log_gamma[:, None, None] * jnp.where(causal, rel, 0.0)[None]), 0.0)
    scores = jnp.einsum('bnihd,bnjhd->bnhij', q, k) * decay[None, None]
    o_intra = jnp.einsum('bnhij,bnjhe->bnihe', scores, v)
    k_dec = k * jnp.exp(log_gamma[None, :] * (L - 1 - idx)[:, None])[None, None, :, :, None]
    kv = jnp.einsum('bnjhd,bnjhe->nbhde', k_dec, v)
    chunk_decay = jnp.exp(log_gamma * L)[None, :, None, None]

    def step(R, kv_n):
        return R * chunk_decay + kv_n, R

    _, R_prev = lax.scan(step, jnp.zeros((B, H, dk, dv), jnp.float32), kv)
    q_dec = q * jnp.exp(log_gamma[None, :] * (idx + 1.0)[:, None])[None, None, :, :, None]
    o_inter = jnp.einsum('bnihd,nbhde->bnihe', q_dec, R_prev)
    return (o_intra + o_inter).reshape(B, S, H, dv)


def _mlstm(q, k, v, i_pre, f_pre):
    B, S, H, dk = q.shape
    dv = v.shape[-1]
    L = CHUNK
    N = S // L
    k = k * (dk ** -0.5)
    q = q.reshape(B, N, L, H, dk)
    k = k.reshape(B, N, L, H, dk)
    v = v.reshape(B, N, L, H, dv)
    log_f = jax.nn.log_sigmoid(f_pre).reshape(B, N, L, H).transpose(0, 1, 3, 2)
    log_i = i_pre.reshape(B, N, L, H).transpose(0, 1, 3, 2)
    b = jnp.cumsum(log_f, axis=-1)
    b_end = b[..., -1]
    causal = jnp.tril(jnp.ones((L, L), dtype=bool))
    log_D = jnp.where(causal, b[..., :, None] - b[..., None, :] + log_i[..., None, :], -jnp.inf)
    log_w_end = b_end[..., None] - b + log_i
    m_loc = log_w_end.max(-1)
    w_end = jnp.exp(log_w_end - m_loc[..., None])
    kv = jnp.einsum('bnhs,bnshd,bnshe->nbhde', w_end, k, v)
    ksum = jnp.einsum('bnhs,bnshd->nbhd', w_end, k)

    def step(carry, inp):
        C, n, m = carry
        kv_n, ks_n, mloc_n, bend_n = inp
        m_new = jnp.maximum(bend_n + m, mloc_n)
        a = jnp.exp(bend_n + m - m_new)
        c = jnp.exp(mloc_n - m_new)
        C_new = a[..., None, None] * C + c[..., None, None] * kv_n
        n_new = a[..., None] * n + c[..., None] * ks_n
        return (C_new, n_new, m_new), (C, n, m)

    init = (jnp.zeros((B, H, dk, dv), jnp.float32), jnp.zeros((B, H, dk), jnp.float32), jnp.zeros((B, H), jnp.float32))
    _, (C_prev, n_prev, m_prev) = lax.scan(
        step, init, (kv, ksum, m_loc.transpose(1, 0, 2), b_end.transpose(1, 0, 2)))
    log_inter = b + m_prev.transpose(1, 0, 2)[..., None]
    m_row = jnp.maximum(log_D.max(-1), log_inter)
    D = jnp.exp(log_D - m_row[..., None])
    inter = jnp.exp(log_inter - m_row)
    qk = jnp.einsum('bnthd,bnshd->bnhts', q, k) * D
    num = (jnp.einsum('bnhts,bnshe->bnthe', qk, v)
           + jnp.einsum('bnthd,nbhde->bnthe', q, C_prev) * inter.transpose(0, 1, 3, 2)[..., None])
    den = qk.sum(-1) + jnp.einsum('bnthd,nbhd->bnht', q, n_prev) * inter
    denom = jnp.maximum(jnp.abs(den), jnp.exp(-m_row)).transpose(0, 1, 3, 2)[..., None]
    return (num / denom).reshape(B, S, H, dv)


def setup_inputs(seed: int = 0) -> dict:
    key = jax.random.key(seed)
    ks = jax.random.split(key, 20)
    f32 = jnp.float32

    def nrm(k, shape, scale):
        return jax.random.normal(k, shape, f32) * scale

    x = nrm(ks[0], (BATCH, SEQ, D_MODEL), 1.0)
    col_scale = jnp.concatenate([jnp.full((n,), c, f32) for n, c in zip(IN_SIZES, IN_BETA)])
    w_in = nrm(ks[1], (DEPTH, D_MODEL, D_IN), D_MODEL ** -0.5) * col_scale
    b_if = jnp.concatenate([
        nrm(ks[2], (DEPTH, MLSTM_HEADS), 0.1),
        jnp.linspace(3.0, 6.0, MLSTM_HEADS, dtype=f32)[None, :] + nrm(ks[3], (DEPTH, MLSTM_HEADS), 0.1)], axis=-1)
    b_merge = nrm(ks[4], (DEPTH, 2 * D_MODEL), 0.02)
    conv_w = nrm(ks[5], (DEPTH, CONV_WIDTH, 2 * ML_QK), CONV_WIDTH ** -0.5)
    conv_b = nrm(ks[6], (DEPTH, 2 * ML_QK), 0.02)
    ret_norm_g = 1.0 + nrm(ks[7], (DEPTH, RET_V), 0.02)
    mlstm_norm_g = 1.0 + nrm(ks[8], (DEPTH, ML_V), 0.02)
    w_proj_ret = nrm(ks[9], (DEPTH, RET_V, D_MODEL), RET_V ** -0.5 * DEEPNORM_BETA)
    w_proj_mlstm = nrm(ks[10], (DEPTH, ML_V, D_MODEL), ML_V ** -0.5 * DEEPNORM_BETA)
    w_out = nrm(ks[11], (DEPTH, D_MODEL, D_MODEL), D_MODEL ** -0.5 * DEEPNORM_BETA)
    ln1_g = 1.0 + nrm(ks[12], (DEPTH, D_MODEL), 0.02)
    ln1_b = nrm(ks[13], (DEPTH, D_MODEL), 0.02)
    w_gate_up = nrm(ks[14], (DEPTH, D_MODEL, 2 * D_FF), D_MODEL ** -0.5)
    w_down = nrm(ks[15], (DEPTH, D_FF, D_MODEL), D_FF ** -0.5 * DEEPNORM_BETA)
    ln2_g = 1.0 + nrm(ks[16], (DEPTH, D_MODEL), 0.02)
    ln2_b = nrm(ks[17], (DEPTH, D_MODEL), 0.02)
    return {"x": x, "w_in": w_in, "b_if": b_if, "b_merge": b_merge, "conv_w": conv_w, "conv_b": conv_b,
            "ret_norm_g": ret_norm_g, "mlstm_norm_g": mlstm_norm_g, "w_proj_ret": w_proj_ret,
            "w_proj_mlstm": w_proj_mlstm, "w_out": w_out, "ln1_g": ln1_g, "ln1_b": ln1_b,
            "w_gate_up": w_gate_up, "w_down": w_down, "ln2_g": ln2_g, "ln2_b": ln2_b}


def reference(x, w_in, b_if, b_merge, conv_w, conv_b, ret_norm_g, mlstm_norm_g, w_proj_ret,
              w_proj_mlstm, w_out, ln1_g, ln1_b, w_gate_up, w_down, ln2_g, ln2_b):
    B, S, _ = x.shape
    dt = x.dtype
    f32 = jnp.float32
    pos = jnp.arange(S, dtype=jnp.int32)
    for l in range(DEPTH):
        proj = x @ w_in[l]
        (r_q, r_k, r_v, r_g, m_qk, m_v, m_o, m_i, m_f, g_a, g_b) = jnp.split(proj, IN_OFFSETS, axis=-1)

        rq = _rotary(r_q.astype(f32).reshape(B, S, RET_HEADS, RET_QK_DIM), pos)
        rk = _rotary(r_k.astype(f32).reshape(B, S, RET_HEADS, RET_QK_DIM), pos) * (RET_QK_DIM ** -0.5)
        rv = r_v.astype(f32).reshape(B, S, RET_HEADS, RET_V_DIM)
        o_ret = _head_norm(_retention(rq, rk, rv), ret_norm_g[l]) * jax.nn.silu(r_g.astype(f32))
        y_ret = o_ret.astype(dt) @ w_proj_ret[l]

        qk_c = jax.nn.silu(_causal_depthwise_conv(m_qk, conv_w[l], conv_b[l])).astype(f32)
        mq = qk_c[..., :ML_QK].reshape(B, S, MLSTM_HEADS, MLSTM_QK_DIM)
        mk = qk_c[..., ML_QK:].reshape(B, S, MLSTM_HEADS, MLSTM_QK_DIM)
        mv = m_v.astype(f32).reshape(B, S, MLSTM_HEADS, MLSTM_V_DIM)
        i_pre = m_i.astype(f32) + b_if[l, :MLSTM_HEADS].astype(f32)
        f_pre = m_f.astype(f32) + b_if[l, MLSTM_HEADS:].astype(f32)
        o_ml = _head_norm(_mlstm(mq, mk, mv, i_pre, f_pre), mlstm_norm_g[l]) * jax.nn.sigmoid(m_o.astype(f32))
        y_ml = o_ml.astype(dt) @ w_proj_mlstm[l]

        gate_a = jax.nn.sigmoid(g_a + b_merge[l, :D_MODEL])
        gate_b = jax.nn.sigmoid(g_b + b_merge[l, D_MODEL:])
        mix = (gate_a * y_ret + gate_b * y_ml) @ w_out[l]
        x = _layer_norm(DEEPNORM_ALPHA * x + mix, ln1_g[l], ln1_b[l])

        gu = x @ w_gate_up[l]
        hidden = jax.nn.silu(gu[..., :D_FF]) * gu[..., D_FF:]
        x = _layer_norm(DEEPNORM_ALPHA * x + hidden @ w_down[l], ln2_g[l], ln2_b[l])
    return x
```

```python
import functools

import jax
import jax.numpy as jnp
import numpy as np
from jax import lax
from jax.experimental import pallas as pl
from jax.experimental.pallas import tpu as pltpu

F32 = jnp.float32
BF16 = jnp.bfloat16

HEADS = 4
QK_DIM = 128
V_DIM = 256
CONV_WIDTH = 4
CHUNK = 128
ROPE_BASE = 10000.0
NORM_EPS = 1e-5
CONV_PAD = 8
GATE_ROWS = 16

MIXER_TILE = 256
FFN_TILE = 512
VMEM_LIMIT = 56 * 1024 * 1024

_QK = HEADS * QK_DIM
_V = HEADS * V_DIM
OFF_RQ = 0
OFF_RK = OFF_RQ + _QK
OFF_RV = OFF_RK + _QK
OFF_RG = OFF_RV + _V
OFF_MQK = OFF_RG + _V
OFF_MV = OFF_MQK + 2 * _QK
OFF_MO = OFF_MV + _V


def _dot(a, b):
    return jnp.dot(a, b, preferred_element_type=F32)


def _dot_nt(a, b):
    return lax.dot_general(a, b, (((1,), (1,)), ((), ())), preferred_element_type=F32)


def _dot_tn(a, b):
    return lax.dot_general(a, b, (((0,), (0,)), ((), ())), preferred_element_type=F32)


def _sigmoid(x):
    return 1.0 / (1.0 + jnp.exp(-x))


def _layer_norm(y, g, b):
    mu = jnp.mean(y, axis=-1, keepdims=True)
    d = y - mu
    var = jnp.mean(d * d, axis=-1, keepdims=True)
    return d * lax.rsqrt(var + NORM_EPS) * g + b


def _head_norm(o):
    mu = jnp.mean(o, axis=-1, keepdims=True)
    d = o - mu
    var = jnp.mean(d * d, axis=-1, keepdims=True)
    return d * lax.rsqrt(var + NORM_EPS)


def _lane_scan(x, lane, op, fill):
    d = 1
    while d < CHUNK:
        x = op(x, jnp.where(lane >= d, pltpu.roll(x, d, 1), fill))
        d *= 2
    return x


def _mixer_kernel(x_ref, cos_ref, sin_ref, wmain_ref, wift_ref, bif_ref, bmerge_ref,
                  convw_ref, convb_ref, rng_ref, mng_ref, wpr_ref, wpm_ref, wo_ref,
                  lng_ref, lnb_ref, dmask_ref, qdec_ref, kdec_ref,
                  o_ref,
                  r_ref, c_ref, n_ref, m_ref, conv_ref, oret_ref, oml_ref,
                  *, tm, d_model, alpha, chunk_decay):
    nc = tm // CHUNK

    @pl.when(pl.program_id(1) == 0)
    def _():
        r_ref[...] = jnp.zeros_like(r_ref)
        c_ref[...] = jnp.zeros_like(c_ref)
        n_ref[...] = jnp.zeros_like(n_ref)
        m_ref[...] = jnp.zeros_like(m_ref)
        conv_ref[0:CONV_PAD, :] = jnp.zeros((CONV_PAD, 2 * _QK), F32)

    x = x_ref[0]
    xb = x.astype(BF16)

    def proj(lo, width):
        return _dot(xb, wmain_ref[:, lo:lo + width])

    cos = cos_ref[...]
    sin = sin_ref[...]
    q_all = proj(OFF_RQ, _QK)
    k_all = proj(OFF_RK, _QK)
    v_all = proj(OFF_RV, _V).astype(BF16)
    g_all = proj(OFF_RG, _V)
    for h in range(HEADS):
        qs = slice(h * QK_DIM, (h + 1) * QK_DIM)
        vs = slice(h * V_DIM, (h + 1) * V_DIM)
        qh = q_all[:, qs]
        kh = k_all[:, qs]
        qh = qh * cos + pltpu.roll(qh, QK_DIM // 2, 1) * sin
        kh = kh * cos + pltpu.roll(kh, QK_DIM // 2, 1) * sin
        for c in range(nc):
            rows = slice(c * CHUNK, (c + 1) * CHUNK)
            qc = qh[rows]
            kc = kh[rows]
            vc = v_all[rows, vs]
            s = _dot_nt(qc.astype(BF16), kc.astype(BF16)) * dmask_ref[h]
            r_state = r_ref[h]
            o = (_dot(s.astype(BF16), vc)
                 + _dot((qc * qdec_ref[h]).astype(BF16), r_state.astype(BF16)))
            r_ref[h] = r_state * chunk_decay[h] + _dot_tn((kc * kdec_ref[h]).astype(BF16), vc)
            gate = g_all[rows, vs]
            out = _head_norm(o) * rng_ref[:, vs] * (gate * _sigmoid(gate))
            oret_ref[rows, vs] = out.astype(BF16)
    y_ret = _dot(oret_ref[...], wpr_ref[...])

    mqk = proj(OFF_MQK, 2 * _QK)
    conv_ref[CONV_PAD:CONV_PAD + tm, :] = mqk
    y = convb_ref[...] + convw_ref[CONV_WIDTH - 1:CONV_WIDTH, :] * mqk
    for j in range(CONV_WIDTH - 1):
        lo = CONV_PAD - (CONV_WIDTH - 1) + j
        y = y + convw_ref[j:j + 1, :] * conv_ref[lo:lo + tm, :]
    conv_ref[0:CONV_PAD, :] = mqk[tm - CONV_PAD:tm, :]
    qk_c = y * _sigmoid(y)
    mv_all = proj(OFF_MV, _V).astype(BF16)
    mo_all = proj(OFF_MO, _V)
    ift = _dot_nt(wift_ref[...], xb) + bif_ref[:, 0:1]

    lane = lax.broadcasted_iota(jnp.int32, (8, CHUNK), 1)
    row_i = lax.broadcasted_iota(jnp.int32, (CHUNK, CHUNK), 0)
    col_i = lax.broadcasted_iota(jnp.int32, (CHUNK, CHUNK), 1)
    causal = col_i <= row_i
    neg_inf = jnp.float32(-jnp.inf)
    pad_rows = jnp.zeros((CHUNK - 32, CHUNK), F32)

    for c in range(nc):
        rows = slice(c * CHUNK, (c + 1) * CHUNK)
        i_pre = ift[0:8, rows]
        f_pre = ift[8:16, rows]
        log_f = jnp.minimum(f_pre, 0.0) - jnp.log1p(jnp.exp(-jnp.abs(f_pre)))
        b = _lane_scan(log_f, lane, jnp.add, 0.0)
        a = i_pre - b
        cm = _lane_scan(a, lane, jnp.maximum, neg_inf)
        m_prev = m_ref[...]
        g = jnp.maximum(cm, m_prev)
        inter = jnp.exp(m_prev - g)
        exp_nm = jnp.exp(-(b + g))
        b_end = b[:, CHUNK - 1:CHUNK]
        cm_end = cm[:, CHUNK - 1:CHUNK]
        m_loc = b_end + cm_end
        w_end = jnp.exp(a - cm_end)
        m_new = jnp.maximum(b_end + m_prev, m_loc)
        a_c = jnp.exp(b_end + m_prev - m_new)
        c_c = jnp.exp(m_loc - m_new)
        m_ref[...] = m_new
        zt = jnp.concatenate([g, inter, exp_nm, w_end, pad_rows], axis=0).T
        n_prev = n_ref[...]
        for h in range(HEADS):
            qs = slice(h * QK_DIM, (h + 1) * QK_DIM)
            vs = slice(h * V_DIM, (h + 1) * V_DIM)
            qc = qk_c[rows, h * QK_DIM:(h + 1) * QK_DIM]
            kc = qk_c[rows, _QK + h * QK_DIM:_QK + (h + 1) * QK_DIM] * (QK_DIM ** -0.5)
            vc = mv_all[rows, vs]
            g_col = zt[:, h:h + 1]
            inter_col = zt[:, 8 + h:9 + h]
            expnm_col = zt[:, 16 + h:17 + h]
            wend_col = zt[:, 24 + h:25 + h]
            dmat = jnp.exp(jnp.where(causal, a[h:h + 1, :] - g_col, neg_inf))
            qcb = qc.astype(BF16)
            qk = _dot_nt(qcb, kc.astype(BF16)) * dmat
            c_state = c_ref[h]
            num = _dot(qk.astype(BF16), vc) + _dot(qcb, c_state.astype(BF16)) * inter_col
            den = (jnp.sum(qk, axis=-1, keepdims=True)
                   + jnp.sum(qc * n_prev[h:h + 1, :], axis=-1, keepdims=True) * inter_col)
            denom = jnp.maximum(jnp.abs(den), expnm_col)
            hid = num * (1.0 / denom)
            kw = kc * wend_col
            kv = _dot_tn(kw.astype(BF16), vc)
            ac_h = a_c[h:h + 1, :]
            cc_h = c_c[h:h + 1, :]
            ac2 = jnp.concatenate([ac_h, ac_h], axis=1)
            cc2 = jnp.concatenate([cc_h, cc_h], axis=1)
            c_ref[h] = ac2 * c_state + cc2 * kv
            n_ref[h:h + 1, :] = ac_h * n_prev[h:h + 1, :] + cc_h * jnp.sum(kw, axis=0, keepdims=True)
            gate = mo_all[rows, vs]
            out = _head_norm(hid) * mng_ref[:, vs] * _sigmoid(gate)
            oml_ref[rows, vs] = out.astype(BF16)
    y_ml = _dot(oml_ref[...], wpm_ref[...])

    off_ga = OFF_MO + _V
    gate_a = _sigmoid(proj(off_ga, d_model) + bmerge_ref[:, 0:d_model])
    gate_b = _sigmoid(proj(off_ga + d_model, d_model) + bmerge_ref[:, d_model:2 * d_model])
    mix = _dot((gate_a * y_ret + gate_b * y_ml).astype(BF16), wo_ref[...])
    o_ref[0] = _layer_norm(alpha * x + mix, lng_ref[...], lnb_ref[...])


def _ffn_kernel(x_ref, wg_ref, wu_ref, wd_ref, lng_ref, lnb_ref, o_ref, *, alpha):
    x = x_ref[...]
    xb = x.astype(BF16)
    gate = _dot(xb, wg_ref[...])
    up = _dot(xb, wu_ref[...])
    hidden = (gate * _sigmoid(gate) * up).astype(BF16)
    y = alpha * x + _dot(hidden, wd_ref[...])
    o_ref[...] = _layer_norm(y, lng_ref[...], lnb_ref[...])


def _const_spec(shape):
    nd = len(shape)
    return pl.BlockSpec(shape, lambda *_: (0,) * nd, pipeline_mode=pl.Buffered(1))


def _mixer_call(x, cos, sin, wmain, wift, bif, bmerge, convw, convb, rng, mng, wpr, wpm, wo,
                lng, lnb, dmask, qdec, kdec, *, alpha, chunk_decay):
    batch, seq, d_model = x.shape
    tm = min(MIXER_TILE, seq)
    consts = (wmain, wift, bif, bmerge, convw, convb, rng, mng, wpr, wpm, wo, lng, lnb,
              dmask, qdec, kdec)
    kern = functools.partial(_mixer_kernel, tm=tm, d_model=d_model, alpha=alpha,
                             chunk_decay=chunk_decay)
    return pl.pallas_call(
        kern,
        grid=(batch, seq // tm),
        in_specs=[pl.BlockSpec((1, tm, d_model), lambda b, j: (b, j, 0)),
                  pl.BlockSpec((tm, QK_DIM), lambda b, j: (j, 0)),
                  pl.BlockSpec((tm, QK_DIM), lambda b, j: (j, 0))]
                 + [_const_spec(c.shape) for c in consts],
        out_specs=pl.BlockSpec((1, tm, d_model), lambda b, j: (b, j, 0)),
        out_shape=jax.ShapeDtypeStruct(x.shape, x.dtype),
        scratch_shapes=[
            pltpu.VMEM((HEADS, QK_DIM, V_DIM), F32),
            pltpu.VMEM((HEADS, QK_DIM, V_DIM), F32),
            pltpu.VMEM((8, QK_DIM), F32),
            pltpu.VMEM((8, CHUNK), F32),
            pltpu.VMEM((CONV_PAD + tm, 2 * _QK), F32),
            pltpu.VMEM((tm, _V), BF16),
            pltpu.VMEM((tm, _V), BF16),
        ],
        compiler_params=pltpu.CompilerParams(
            dimension_semantics=("arbitrary", "arbitrary"), vmem_limit_bytes=VMEM_LIMIT),
        name="mixer",
    )(x, cos, sin, *consts)


def _ffn_call(x2d, wg, wu, wd, lng, lnb, *, alpha):
    tokens, d_model = x2d.shape
    tm = min(FFN_TILE, tokens)
    consts = (wg, wu, wd, lng, lnb)
    return pl.pallas_call(
        functools.partial(_ffn_kernel, alpha=alpha),
        grid=(tokens // tm,),
        in_specs=[pl.BlockSpec((tm, d_model), lambda i: (i, 0))]
                 + [_const_spec(c.shape) for c in consts],
        out_specs=pl.BlockSpec((tm, d_model), lambda i: (i, 0)),
        out_shape=jax.ShapeDtypeStruct(x2d.shape, x2d.dtype),
        compiler_params=pltpu.CompilerParams(
            dimension_semantics=("arbitrary",), vmem_limit_bytes=VMEM_LIMIT),
        name="ffn",
    )(x2d, *consts)


def _position_tables(seq):
    half = QK_DIM // 2
    inv_freq = ROPE_BASE ** (-jnp.arange(half, dtype=F32) / half)
    ang = jnp.arange(seq, dtype=jnp.int32).astype(F32)[:, None] * inv_freq[None, :]
    cos = jnp.cos(ang)
    sin = jnp.sin(ang)
    return jnp.concatenate([cos, cos], axis=-1), jnp.concatenate([-sin, sin], axis=-1)


def _decay_tables():
    scale = QK_DIM ** -0.5
    log_gamma = jnp.log(1.0 - jnp.power(2.0, -5.0 - jnp.arange(HEADS, dtype=F32)))
    idx = jnp.arange(CHUNK, dtype=F32)
    rel = idx[:, None] - idx[None, :]
    causal = rel >= 0
    dmask = jnp.where(causal[None],
                      jnp.exp(log_gamma[:, None, None] * jnp.where(causal, rel, 0.0)[None]), 0.0) * scale
    qdec = jnp.exp(log_gamma[:, None] * (idx + 1.0)[None, :])
    kdec = jnp.exp(log_gamma[:, None] * (CHUNK - 1 - idx)[None, :]) * scale
    qdec = jnp.broadcast_to(qdec[:, :, None], (HEADS, CHUNK, QK_DIM))
    kdec = jnp.broadcast_to(kdec[:, :, None], (HEADS, CHUNK, QK_DIM))
    log_gamma64 = np.log(1.0 - np.power(2.0, -5.0 - np.arange(HEADS, dtype=np.float64)))
    chunk_decay = tuple(float(v) for v in np.exp(log_gamma64 * CHUNK))
    return dmask, qdec, kdec, chunk_decay


def kernel(x, w_in, b_if, b_merge, conv_w, conv_b, ret_norm_g, mlstm_norm_g, w_proj_ret,
           w_proj_mlstm, w_out, ln1_g, ln1_b, w_gate_up, w_down, ln2_g, ln2_b):
    batch, seq, d_model = x.shape
    depth = w_in.shape[0]
    d_ff = w_down.shape[1]
    alpha = float((2 * depth) ** 0.25)
    assert seq % CHUNK == 0 and d_model % 128 == 0

    cos, sin = _position_tables(seq)
    dmask, qdec, kdec, chunk_decay = _decay_tables()

    off_i = OFF_MO + _V
    w_main = jnp.concatenate([w_in[:, :, :off_i], w_in[:, :, off_i + 2 * HEADS:]], axis=-1).astype(BF16)
    w_if = jnp.swapaxes(w_in[:, :, off_i:off_i + 2 * HEADS], 1, 2)
    zrow = jnp.zeros((depth, 8 - HEADS, d_model), w_if.dtype)
    w_ift = jnp.concatenate([w_if[:, :HEADS], zrow, w_if[:, HEADS:], zrow], axis=1).astype(BF16)
    zb = jnp.zeros((depth, 8 - HEADS), b_if.dtype)
    bif = jnp.concatenate([b_if[:, :HEADS], zb, b_if[:, HEADS:], zb], axis=1)
    bif = jnp.broadcast_to(bif[:, :, None], (depth, GATE_ROWS, 128)).astype(F32)

    w_pr = w_proj_ret.astype(BF16)
    w_pm = w_proj_mlstm.astype(BF16)
    w_o = w_out.astype(BF16)
    w_g = w_gate_up[:, :, :d_ff].astype(BF16)
    w_u = w_gate_up[:, :, d_ff:].astype(BF16)
    w_d = w_down.astype(BF16)

    def row(v):
        return v.reshape(1, -1).astype(F32)

    for l in range(depth):
        x = _mixer_call(x, cos, sin, w_main[l], w_ift[l], bif[l], row(b_merge[l]), conv_w[l].astype(F32),
                        row(conv_b[l]), row(ret_norm_g[l]), row(mlstm_norm_g[l]), w_pr[l], w_pm[l],
                        w_o[l], row(ln1_g[l]), row(ln1_b[l]), dmask, qdec, kdec,
                        alpha=alpha, chunk_decay=chunk_decay)
        x2 = _ffn_call(x.reshape(batch * seq, d_model), w_g[l], w_u[l], w_d[l], row(ln2_g[l]),
                       row(ln2_b[l]), alpha=alpha)
        x = x2.reshape(batch, seq, d_model)
    return x
```

```python
import functools

import jax
import jax.numpy as jnp
import numpy as np
from jax import lax
from jax.experimental import pallas as pl
from jax.experimental.pallas import tpu as pltpu

F32 = jnp.float32
BF16 = jnp.bfloat16

HEADS = 4
QK_DIM = 128
V_DIM = 256
CONV_WIDTH = 4
ROPE_BASE = 10000.0
NORM_EPS = 1e-5
CONV_PAD = 8
GATE_ROWS = 16

MIXER_TILE = 256
FFN_TILE = 512
VMEM_LIMIT = 56 * 1024 * 1024

_QK = HEADS * QK_DIM
_V = HEADS * V_DIM
OFF_RQ = 0
OFF_RK = OFF_RQ + _QK
OFF_RV = OFF_RK + _QK
OFF_RG = OFF_RV + _V
OFF_MQK = OFF_RG + _V
OFF_MV = OFF_MQK + 2 * _QK
OFF_MO = OFF_MV + _V
OFF_GA = OFF_MO + _V
P_Q = 0
P_K = P_Q + _QK
P_G = P_K + _QK
P_MQK = P_G + _V
P_MO = P_MQK + 2 * _QK
P_WIDTH = P_MO + _V


def _dot(a, b):
    return jnp.dot(a, b, preferred_element_type=F32)


def _dot_nt(a, b):
    return lax.dot_general(a, b, (((1,), (1,)), ((), ())), preferred_element_type=F32)


def _dot_tn(a, b):
    return lax.dot_general(a, b, (((0,), (0,)), ((), ())), preferred_element_type=F32)


def _sigmoid(x):
    return 1.0 / (1.0 + jnp.exp(-x))


def _layer_norm(y, g, b):
    mu = jnp.mean(y, axis=-1, keepdims=True)
    d = y - mu
    var = jnp.mean(d * d, axis=-1, keepdims=True)
    return d * lax.rsqrt(var + NORM_EPS) * g + b


def _head_norm(o):
    mu = jnp.mean(o, axis=-1, keepdims=True)
    d = o - mu
    var = jnp.mean(d * d, axis=-1, keepdims=True)
    return d * lax.rsqrt(var + NORM_EPS)


def _lane_scan(x, lane, op, fill, width):
    d = 1
    while d < width:
        x = op(x, jnp.where(lane >= d, pltpu.roll(x, d, 1), fill))
        d *= 2
    return x


def _project_pieces(x_ref, wmain_ref, wift_ref, bif_ref, p32_ref, p16_ref, pif_ref):
    xb = x_ref[0].astype(BF16)

    def f32_piece(dst, src, width):
        def run():
            p32_ref[:, dst:dst + width] = _dot(xb, wmain_ref[:, src:src + width])
        return run

    def bf16_piece(dst, src, width):
        def run():
            p16_ref[:, dst:dst + width] = _dot(xb, wmain_ref[:, src:src + width]).astype(BF16)
        return run

    def gate_rows():
        pif_ref[...] = _dot_nt(wift_ref[...], xb) + bif_ref[:, 0:1]

    return [f32_piece(P_Q, OFF_RQ, 2 * _QK),
            bf16_piece(0, OFF_RV, _V),
            f32_piece(P_G, OFF_RG, _V),
            f32_piece(P_MQK, OFF_MQK, 2 * _QK),
            bf16_piece(_V, OFF_MV, _V),
            f32_piece(P_MO, OFF_MO, _V),
            gate_rows]


def _stage_mix(x_ref, cos, sin, p32_ref, p16_ref, pif_ref, wmain_ref, bmerge_ref, convw_ref,
               convb_ref, rng_ref, mng_ref, wpr_ref, wpm_ref, wo_ref, lng_ref, lnb_ref,
               dmask_ref, qdec_ref, kdec_ref, r_ref, c_ref, n_ref, m_ref, conv_ref, oret_ref,
               oml_ref, *, tm, d_model, alpha, chunk_decay, side_work=()):
    side_work = list(side_work)

    def side():
        if side_work:
            side_work.pop(0)()

    lane = lax.broadcasted_iota(jnp.int32, (8, tm), 1)
    row_i = lax.broadcasted_iota(jnp.int32, (tm, tm), 0)
    col_i = lax.broadcasted_iota(jnp.int32, (tm, tm), 1)
    causal = col_i <= row_i
    neg_inf = jnp.float32(-jnp.inf)

    m_prev = m_ref[:, 0:1]
    i_pre = pif_ref[0:8, :]
    f_pre = pif_ref[8:16, :]
    log_f = jnp.minimum(f_pre, 0.0) - jnp.log1p(jnp.exp(-jnp.abs(f_pre)))
    b = _lane_scan(log_f, lane, jnp.add, 0.0, tm)
    a = i_pre - b
    cm = _lane_scan(a, lane, jnp.maximum, neg_inf, tm)
    g = jnp.maximum(cm, m_prev)
    inter = jnp.exp(m_prev - g)
    exp_nm = jnp.exp(-(b + g))
    b_end = b[:, tm - 1:tm]
    cm_end = cm[:, tm - 1:tm]
    m_loc = b_end + cm_end
    w_end = jnp.exp(a - cm_end)
    m_new = jnp.maximum(b_end + m_prev, m_loc)
    a_c = jnp.broadcast_to(jnp.exp(b_end + m_prev - m_new), (8, V_DIM))
    c_c = jnp.broadcast_to(jnp.exp(m_loc - m_new), (8, V_DIM))
    m_ref[...] = jnp.broadcast_to(m_new, m_ref.shape)
    pad_rows = jnp.zeros((128 - 32, tm), F32)
    zt = jnp.concatenate([g, inter, exp_nm, w_end, pad_rows], axis=0).T

    for h in range(HEADS):
        side()
        vs = slice(h * V_DIM, (h + 1) * V_DIM)
        qc = p32_ref[:, P_Q + h * QK_DIM:P_Q + (h + 1) * QK_DIM]
        kc = p32_ref[:, P_K + h * QK_DIM:P_K + (h + 1) * QK_DIM]
        qc = qc * cos + pltpu.roll(qc, QK_DIM // 2, 1) * sin
        kc = kc * cos + pltpu.roll(kc, QK_DIM // 2, 1) * sin
        vc = p16_ref[:, vs]
        s = _dot_nt(qc.astype(BF16), kc.astype(BF16)) * dmask_ref[h]
        r_state = r_ref[h]
        o = (_dot(s.astype(BF16), vc)
             + _dot((qc * qdec_ref[h]).astype(BF16), r_state.astype(BF16)))
        r_ref[h] = r_state * chunk_decay[h] + _dot_tn((kc * kdec_ref[h]).astype(BF16), vc)
        gate = p32_ref[:, P_G + h * V_DIM:P_G + (h + 1) * V_DIM]
        out = _head_norm(o) * rng_ref[:, vs] * (gate * _sigmoid(gate))
        oret_ref[:, vs] = out.astype(BF16)
    y_ret = _dot(oret_ref[...], wpr_ref[...])

    mqk = p32_ref[:, P_MQK:P_MQK + 2 * _QK]
    conv_ref[CONV_PAD:CONV_PAD + tm, :] = mqk
    y = convb_ref[...] + convw_ref[CONV_WIDTH - 1:CONV_WIDTH, :] * mqk
    for j in range(CONV_WIDTH - 1):
        lo = CONV_PAD - (CONV_WIDTH - 1) + j
        y = y + convw_ref[j:j + 1, :] * conv_ref[lo:lo + tm, :]
    conv_ref[0:CONV_PAD, :] = mqk[tm - CONV_PAD:tm, :]
    qk_c = y * _sigmoid(y)

    x = x_ref[0]
    xb = x.astype(BF16)

    def merge_gate(k):
        pre = _dot(xb, wmain_ref[:, OFF_GA + k * d_model:OFF_GA + (k + 1) * d_model])
        return _sigmoid(pre + bmerge_ref[:, k * d_model:(k + 1) * d_model])

    merge_gates = []
    n_prev = n_ref[...]
    for h in range(HEADS):
        side()
        if h >= 2:
            merge_gates.append(merge_gate(h - 2))
        vs = slice(h * V_DIM, (h + 1) * V_DIM)
        qc = qk_c[:, h * QK_DIM:(h + 1) * QK_DIM]
        kc = qk_c[:, _QK + h * QK_DIM:_QK + (h + 1) * QK_DIM] * (QK_DIM ** -0.5)
        vc = p16_ref[:, _V + h * V_DIM:_V + (h + 1) * V_DIM]
        g_col = zt[:, h:h + 1]
        inter_col = zt[:, 8 + h:9 + h]
        expnm_col = zt[:, 16 + h:17 + h]
        wend_col = zt[:, 24 + h:25 + h]
        dmat = jnp.exp(jnp.where(causal, a[h:h + 1, :] - g_col, neg_inf))
        qcb = qc.astype(BF16)
        qk = _dot_nt(qcb, kc.astype(BF16)) * dmat
        c_state = c_ref[h]
        num = _dot(qk.astype(BF16), vc) + _dot(qcb, c_state.astype(BF16)) * inter_col
        den = (jnp.sum(qk, axis=-1, keepdims=True)
               + jnp.sum(qc * n_prev[h:h + 1, :], axis=-1, keepdims=True) * inter_col)
        denom = jnp.maximum(jnp.abs(den), expnm_col)
        hid = num * (1.0 / denom)
        kw = kc * wend_col
        kv = _dot_tn(kw.astype(BF16), vc)
        ac_h = a_c[h:h + 1, :]
        cc_h = c_c[h:h + 1, :]
        c_ref[h] = ac_h * c_state + cc_h * kv
        n_ref[h:h + 1, :] = (ac_h[:, 0:QK_DIM] * n_prev[h:h + 1, :]
                             + cc_h[:, 0:QK_DIM] * jnp.sum(kw, axis=0, keepdims=True))
        gate = p32_ref[:, P_MO + h * V_DIM:P_MO + (h + 1) * V_DIM]
        out = _head_norm(hid) * mng_ref[:, vs] * _sigmoid(gate)
        oml_ref[:, vs] = out.astype(BF16)
    y_ml = _dot(oml_ref[...], wpm_ref[...])

    while side_work:
        side()
    gate_a, gate_b = merge_gates
    mix = _dot((gate_a * y_ret + gate_b * y_ml).astype(BF16), wo_ref[...])
    return _layer_norm(alpha * x + mix, lng_ref[...], lnb_ref[...])


def _mixer_kernel(x_ref, xn_ref, cos_ref, sin_ref, wmain_ref, wift_ref, bif_ref,
                  bmerge_ref, convw_ref, convb_ref, rng_ref, mng_ref, wpr_ref, wpm_ref, wo_ref,
                  lng_ref, lnb_ref, dmask_ref, qdec_ref, kdec_ref,
                  o_ref,
                  r_ref, c_ref, n_ref, m_ref, conv_ref, oret_ref, oml_ref,
                  pa32_ref, pa16_ref, paif_ref, pb32_ref, pb16_ref, pbif_ref,
                  *, tm, d_model, alpha, chunk_decay):
    step_id = pl.program_id(1)
    pieces = functools.partial(_project_pieces, wmain_ref=wmain_ref, wift_ref=wift_ref,
                               bif_ref=bif_ref)
    buf_a = dict(p32_ref=pa32_ref, p16_ref=pa16_ref, pif_ref=paif_ref)
    buf_b = dict(p32_ref=pb32_ref, p16_ref=pb16_ref, pif_ref=pbif_ref)
    mix = functools.partial(
        _stage_mix, wmain_ref=wmain_ref, bmerge_ref=bmerge_ref, convw_ref=convw_ref,
        convb_ref=convb_ref, rng_ref=rng_ref, mng_ref=mng_ref, wpr_ref=wpr_ref, wpm_ref=wpm_ref,
        wo_ref=wo_ref, lng_ref=lng_ref, lnb_ref=lnb_ref, dmask_ref=dmask_ref, qdec_ref=qdec_ref,
        kdec_ref=kdec_ref, r_ref=r_ref, c_ref=c_ref, n_ref=n_ref, m_ref=m_ref, conv_ref=conv_ref,
        oret_ref=oret_ref, oml_ref=oml_ref, tm=tm, d_model=d_model, alpha=alpha,
        chunk_decay=chunk_decay)

    @pl.when(step_id == 0)
    def _():
        r_ref[...] = jnp.zeros_like(r_ref)
        c_ref[...] = jnp.zeros_like(c_ref)
        n_ref[...] = jnp.zeros_like(n_ref)
        m_ref[...] = jnp.zeros_like(m_ref)
        conv_ref[0:CONV_PAD, :] = jnp.zeros((CONV_PAD, 2 * _QK), F32)
        for piece in pieces(x_ref, **buf_a):
            piece()

    @pl.when(step_id % 2 == 0)
    def _():
        o_ref[0] = mix(x_ref, cos_ref[...], sin_ref[...], **buf_a, side_work=pieces(xn_ref, **buf_b))

    @pl.when(step_id % 2 == 1)
    def _():
        o_ref[0] = mix(x_ref, cos_ref[...], sin_ref[...], **buf_b, side_work=pieces(xn_ref, **buf_a))


def _ffn_kernel(x_ref, wg_ref, wu_ref, wd_ref, lng_ref, lnb_ref, o_ref, *, alpha):
    x = x_ref[...]
    xb = x.astype(BF16)
    gate = _dot(xb, wg_ref[...])
    up = _dot(xb, wu_ref[...])
    hidden = (gate * _sigmoid(gate) * up).astype(BF16)
    y = alpha * x + _dot(hidden, wd_ref[...])
    o_ref[...] = _layer_norm(y, lng_ref[...], lnb_ref[...])


def _const_spec(shape):
    nd = len(shape)
    return pl.BlockSpec(shape, lambda *_: (0,) * nd, pipeline_mode=pl.Buffered(1))


def _mixer_call(x, cos, sin, wmain, wift, bif, bmerge, convw, convb, rng, mng, wpr, wpm, wo,
                lng, lnb, dmask, qdec, kdec, *, alpha, chunk_decay):
    batch, seq, d_model = x.shape
    tm = MIXER_TILE
    assert seq % tm == 0
    n_tiles = seq // tm
    consts = (wmain, wift, bif, bmerge, convw, convb, rng, mng, wpr, wpm, wo, lng, lnb,
              dmask, qdec, kdec)
    kern = functools.partial(_mixer_kernel, tm=tm, d_model=d_model, alpha=alpha,
                             chunk_decay=chunk_decay)
    handoff = [pltpu.VMEM((tm, P_WIDTH), F32),
               pltpu.VMEM((tm, 2 * _V), BF16),
               pltpu.VMEM((GATE_ROWS, tm), F32)]
    return pl.pallas_call(
        kern,
        grid=(batch, n_tiles),
        in_specs=[pl.BlockSpec((1, tm, d_model), lambda b, j: (b, j, 0)),
                  pl.BlockSpec((1, tm, d_model),
                               lambda b, j: (b, jnp.minimum(j + 1, n_tiles - 1), 0)),
                  pl.BlockSpec((tm, QK_DIM), lambda b, j: (j, 0)),
                  pl.BlockSpec((tm, QK_DIM), lambda b, j: (j, 0))]
                 + [_const_spec(c.shape) for c in consts],
        out_specs=pl.BlockSpec((1, tm, d_model), lambda b, j: (b, j, 0)),
        out_shape=jax.ShapeDtypeStruct(x.shape, x.dtype),
        scratch_shapes=[
            pltpu.VMEM((HEADS, QK_DIM, V_DIM), F32),
            pltpu.VMEM((HEADS, QK_DIM, V_DIM), F32),
            pltpu.VMEM((8, QK_DIM), F32),
            pltpu.VMEM((8, 128), F32),
            pltpu.VMEM((CONV_PAD + tm, 2 * _QK), F32),
            pltpu.VMEM((tm, _V), BF16),
            pltpu.VMEM((tm, _V), BF16),
        ] + handoff + handoff,
        compiler_params=pltpu.CompilerParams(
            dimension_semantics=("arbitrary", "arbitrary"), vmem_limit_bytes=VMEM_LIMIT),
        name="mixer",
    )(x, x, cos, sin, *consts)


def _ffn_call(x2d, wg, wu, wd, lng, lnb, *, alpha):
    tokens, d_model = x2d.shape
    tm = min(FFN_TILE, tokens)
    consts = (wg, wu, wd, lng, lnb)
    return pl.pallas_call(
        functools.partial(_ffn_kernel, alpha=alpha),
        grid=(tokens // tm,),
        in_specs=[pl.BlockSpec((tm, d_model), lambda i: (i, 0))]
                 + [_const_spec(c.shape) for c in consts],
        out_specs=pl.BlockSpec((tm, d_model), lambda i: (i, 0)),
        out_shape=jax.ShapeDtypeStruct(x2d.shape, x2d.dtype),
        compiler_params=pltpu.CompilerParams(
            dimension_semantics=("arbitrary",), vmem_limit_bytes=VMEM_LIMIT),
        name="ffn",
    )(x2d, *consts)


def _position_tables(seq):
    half = QK_DIM // 2
    inv_freq = ROPE_BASE ** (-jnp.arange(half, dtype=F32) / half)
    ang = jnp.arange(seq, dtype=jnp.int32).astype(F32)[:, None] * inv_freq[None, :]
    cos = jnp.cos(ang)
    sin = jnp.sin(ang)
    return jnp.concatenate([cos, cos], axis=-1), jnp.concatenate([-sin, sin], axis=-1)


def _decay_tables(chunk):
    scale = QK_DIM ** -0.5
    log_gamma = jnp.log(1.0 - jnp.power(2.0, -5.0 - jnp.arange(HEADS, dtype=F32)))
    idx = jnp.arange(chunk, dtype=F32)
    rel = idx[:, None] - idx[None, :]
    causal = rel >= 0
    dmask = jnp.where(causal[None],
                      jnp.exp(log_gamma[:, None, None] * jnp.where(causal, rel, 0.0)[None]), 0.0) * scale
    qdec = jnp.exp(log_gamma[:, None] * (idx + 1.0)[None, :])
    kdec = jnp.exp(log_gamma[:, None] * (chunk - 1 - idx)[None, :]) * scale
    qdec = jnp.broadcast_to(qdec[:, :, None], (HEADS, chunk, QK_DIM))
    kdec = jnp.broadcast_to(kdec[:, :, None], (HEADS, chunk, QK_DIM))
    log_gamma64 = np.log(1.0 - np.power(2.0, -5.0 - np.arange(HEADS, dtype=np.float64)))
    chunk_decay = tuple(float(v) for v in np.exp(log_gamma64 * chunk))
    return dmask, qdec, kdec, chunk_decay


def kernel(x, w_in, b_if, b_merge, conv_w, conv_b, ret_norm_g, mlstm_norm_g, w_proj_ret,
           w_proj_mlstm, w_out, ln1_g, ln1_b, w_gate_up, w_down, ln2_g, ln2_b):
    batch, seq, d_model = x.shape
    depth = w_in.shape[0]
    d_ff = w_down.shape[1]
    alpha = float((2 * depth) ** 0.25)
    assert seq % MIXER_TILE == 0 and d_model % 128 == 0

    cos, sin = _position_tables(seq)
    dmask, qdec, kdec, chunk_decay = _decay_tables(MIXER_TILE)

    off_i = OFF_MO + _V
    w_main = jnp.concatenate([w_in[:, :, :off_i], w_in[:, :, off_i + 2 * HEADS:]], axis=-1).astype(BF16)
    w_if = jnp.swapaxes(w_in[:, :, off_i:off_i + 2 * HEADS], 1, 2)
    zrow = jnp.zeros((depth, 8 - HEADS, d_model), w_if.dtype)
    w_ift = jnp.concatenate([w_if[:, :HEADS], zrow, w_if[:, HEADS:], zrow], axis=1).astype(BF16)
    zb = jnp.zeros((depth, 8 - HEADS), b_if.dtype)
    bif = jnp.concatenate([b_if[:, :HEADS], zb, b_if[:, HEADS:], zb], axis=1)
    bif = jnp.broadcast_to(bif[:, :, None], (depth, GATE_ROWS, 128)).astype(F32)

    w_pr = w_proj_ret.astype(BF16)
    w_pm = w_proj_mlstm.astype(BF16)
    w_o = w_out.astype(BF16)
    w_g = w_gate_up[:, :, :d_ff].astype(BF16)
    w_u = w_gate_up[:, :, d_ff:].astype(BF16)
    w_d = w_down.astype(BF16)

    def row(v):
        return v.reshape(1, -1).astype(F32)

    for l in range(depth):
        x = _mixer_call(x, cos, sin, w_main[l], w_ift[l], bif[l], row(b_merge[l]), conv_w[l].astype(F32),
                        row(conv_b[l]), row(ret_norm_g[l]), row(mlstm_norm_g[l]), w_pr[l], w_pm[l],
                        w_o[l], row(ln1_g[l]), row(ln1_b[l]), dmask, qdec, kdec,
                        alpha=alpha, chunk_decay=chunk_decay)
        x2 = _ffn_call(x.reshape(batch * seq, d_model), w_g[l], w_u[l], w_d[l], row(ln2_g[l]),
                       row(ln2_b[l]), alpha=alpha)
        x = x2.reshape(batch, seq, d_model)
    return x
```

```python
import functools

import jax
import jax.numpy as jnp
import numpy as np
from jax import lax
from jax.experimental import pallas as pl
from jax.experimental.pallas import tpu as pltpu

F32 = jnp.float32
BF16 = jnp.bfloat16

HEADS = 4
QK_DIM = 128
V_DIM = 256
CONV_WIDTH = 4
ROPE_BASE = 10000.0
NORM_EPS = 1e-5
CONV_PAD = 8
GATE_ROWS = 16
GATE_COLS = 128

MIXER_TILE = 256
FFN_TILE = 1024
FFN_SUB = 256
VMEM_LIMIT = 56 * 1024 * 1024

_QK = HEADS * QK_DIM
_V = HEADS * V_DIM
OFF_RQ = 0
OFF_RK = OFF_RQ + _QK
OFF_RV = OFF_RK + _QK
OFF_RG = OFF_RV + _V
OFF_MQK = OFF_RG + _V
OFF_MV = OFF_MQK + 2 * _QK
OFF_MO = OFF_MV + _V
OFF_GA = OFF_MO + _V
P_Q = 0
P_K = P_Q + _QK
P_G = P_K + _QK
P_MQK = P_G + _V
P_MO = P_MQK + 2 * _QK
P_WIDTH = P_MO + _V


def _dot(a, b):
    return jnp.dot(a, b, preferred_element_type=F32)


def _sigmoid(x):
    return 1.0 / (1.0 + jnp.exp(-x))


def _layer_norm(y, g, b):
    mu = jnp.mean(y, axis=-1, keepdims=True)
    d = y - mu
    var = jnp.mean(d * d, axis=-1, keepdims=True)
    return d * lax.rsqrt(var + NORM_EPS) * g + b


def _head_norm(o):
    mu = jnp.mean(o, axis=-1, keepdims=True)
    d = o - mu
    var = jnp.mean(d * d, axis=-1, keepdims=True)
    return d * lax.rsqrt(var + NORM_EPS)


def _lane_scan(x, lane, op, fill, width):
    d = 1
    while d < width:
        x = op(x, jnp.where(lane >= d, pltpu.roll(x, d, 1), fill))
        d *= 2
    return x


def _project_pieces(x_ref, wmain_ref, wif_ref, bif_ref, p32_ref, p16_ref, pif_ref):
    xb = x_ref[0].astype(BF16)

    def f32_piece(dst, src, width):
        def run():
            p32_ref[:, dst:dst + width] = _dot(xb, wmain_ref[:, src:src + width])
        return run

    def bf16_piece(dst, src, width):
        def run():
            p16_ref[:, dst:dst + width] = _dot(xb, wmain_ref[:, src:src + width]).astype(BF16)
        return run

    def gate_rows():
        pre = _dot(xb, wif_ref[...]) + bif_ref[...]
        pif_ref[...] = pre.T[0:GATE_ROWS, :]

    return [f32_piece(P_Q, OFF_RQ, 2 * _QK),
            bf16_piece(0, OFF_RV, _V),
            f32_piece(P_G, OFF_RG, _V),
            f32_piece(P_MQK, OFF_MQK, 2 * _QK),
            bf16_piece(_V, OFF_MV, _V),
            f32_piece(P_MO, OFF_MO, _V),
            gate_rows]


def _stage_mix(x_ref, cos, sin, p32_ref, p16_ref, pif_ref, wmain_ref, bmerge_ref, convw_ref,
               convb_ref, rng_ref, mng_ref, wpr_ref, wpm_ref, wo_ref, lng_ref, lnb_ref,
               dmask_ref, qdec_ref, kdec_ref, r_ref, c_ref, n_ref, m_ref, conv_ref, oret_ref,
               oml_ref, *, tm, d_model, alpha, chunk_decay, side_work=()):
    side_work = list(side_work)

    def side():
        if side_work:
            side_work.pop(0)()

    lane = lax.broadcasted_iota(jnp.int32, (8, tm), 1)
    row_i = lax.broadcasted_iota(jnp.int32, (tm, tm), 0)
    col_i = lax.broadcasted_iota(jnp.int32, (tm, tm), 1)
    causal = col_i <= row_i
    neg_inf = jnp.float32(-jnp.inf)

    m_prev = m_ref[:, 0:1]
    i_pre = pif_ref[0:8, :]
    f_pre = pif_ref[8:16, :]
    log_f = jnp.minimum(f_pre, 0.0) - jnp.log1p(jnp.exp(-jnp.abs(f_pre)))
    b = _lane_scan(log_f, lane, jnp.add, 0.0, tm)
    a = i_pre - b
    cm = _lane_scan(a, lane, jnp.maximum, neg_inf, tm)
    g = jnp.maximum(cm, m_prev)
    inter = jnp.exp(m_prev - g)
    exp_nm = jnp.exp(-(b + g))
    b_end = b[:, tm - 1:tm]
    cm_end = cm[:, tm - 1:tm]
    m_loc = b_end + cm_end
    w_end = jnp.exp(a - cm_end)
    m_new = jnp.maximum(b_end + m_prev, m_loc)
    a_c = jnp.broadcast_to(jnp.exp(b_end + m_prev - m_new), (8, V_DIM))
    c_c = jnp.broadcast_to(jnp.exp(m_loc - m_new), (8, V_DIM))
    m_ref[...] = jnp.broadcast_to(m_new, m_ref.shape)
    pad_rows = jnp.zeros((128 - 32, tm), F32)
    zt = jnp.concatenate([g, inter, exp_nm, w_end, pad_rows], axis=0).T

    for h in range(HEADS):
        side()
        vs = slice(h * V_DIM, (h + 1) * V_DIM)
        qc = p32_ref[:, P_Q + h * QK_DIM:P_Q + (h + 1) * QK_DIM]
        kc = p32_ref[:, P_K + h * QK_DIM:P_K + (h + 1) * QK_DIM]
        qc = qc * cos + pltpu.roll(qc, QK_DIM // 2, 1) * sin
        kc = kc * cos + pltpu.roll(kc, QK_DIM // 2, 1) * sin
        vc = p16_ref[:, vs]
        kt = kc.T
        s = _dot(qc.astype(BF16), kt.astype(BF16)) * dmask_ref[h]
        r_state = r_ref[h]
        o = (_dot(s.astype(BF16), vc)
             + _dot((qc * qdec_ref[h]).astype(BF16), r_state.astype(BF16)))
        r_ref[h] = r_state * chunk_decay[h] + _dot((kt * kdec_ref[h]).astype(BF16), vc)
        gate = p32_ref[:, P_G + h * V_DIM:P_G + (h + 1) * V_DIM]
        out = _head_norm(o) * rng_ref[:, vs] * (gate * _sigmoid(gate))
        oret_ref[:, vs] = out.astype(BF16)
    y_ret = _dot(oret_ref[...], wpr_ref[...])

    mqk = p32_ref[:, P_MQK:P_MQK + 2 * _QK]
    conv_ref[CONV_PAD:CONV_PAD + tm, :] = mqk
    y = convb_ref[...] + convw_ref[CONV_WIDTH - 1:CONV_WIDTH, :] * mqk
    for j in range(CONV_WIDTH - 1):
        lo = CONV_PAD - (CONV_WIDTH - 1) + j
        y = y + convw_ref[j:j + 1, :] * conv_ref[lo:lo + tm, :]
    conv_ref[0:CONV_PAD, :] = mqk[tm - CONV_PAD:tm, :]
    qk_c = y * _sigmoid(y)

    x = x_ref[0]
    xb = x.astype(BF16)

    def merge_gate(k):
        pre = _dot(xb, wmain_ref[:, OFF_GA + k * d_model:OFF_GA + (k + 1) * d_model])
        return _sigmoid(pre + bmerge_ref[:, k * d_model:(k + 1) * d_model])

    merge_gates = []
    n_prev = n_ref[...]
    for h in range(HEADS):
        side()
        if h >= 2:
            merge_gates.append(merge_gate(h - 2))
        vs = slice(h * V_DIM, (h + 1) * V_DIM)
        qc = qk_c[:, h * QK_DIM:(h + 1) * QK_DIM]
        kc = qk_c[:, _QK + h * QK_DIM:_QK + (h + 1) * QK_DIM] * (QK_DIM ** -0.5)
        vc = p16_ref[:, _V + h * V_DIM:_V + (h + 1) * V_DIM]
        g_col = zt[:, h:h + 1]
        inter_col = zt[:, 8 + h:9 + h]
        expnm_col = zt[:, 16 + h:17 + h]
        wend_col = zt[:, 24 + h:25 + h]
        dmat = jnp.exp(jnp.where(causal, a[h:h + 1, :] - g_col, neg_inf))
        qcb = qc.astype(BF16)
        kt = kc.T
        qk = _dot(qcb, kt.astype(BF16)) * dmat
        c_state = c_ref[h]
        num = _dot(qk.astype(BF16), vc) + _dot(qcb, c_state.astype(BF16)) * inter_col
        den = (jnp.sum(qk, axis=-1, keepdims=True)
               + jnp.sum(qc * n_prev[h:h + 1, :], axis=-1, keepdims=True) * inter_col)
        denom = jnp.maximum(jnp.abs(den), expnm_col)
        hid = num * (1.0 / denom)
        kw = kc * wend_col
        kv = _dot((kt * w_end[h:h + 1, :]).astype(BF16), vc)
        ac_h = a_c[h:h + 1, :]
        cc_h = c_c[h:h + 1, :]
        c_ref[h] = ac_h * c_state + cc_h * kv
        n_ref[h:h + 1, :] = (ac_h[:, 0:QK_DIM] * n_prev[h:h + 1, :]
                             + cc_h[:, 0:QK_DIM] * jnp.sum(kw, axis=0, keepdims=True))
        gate = p32_ref[:, P_MO + h * V_DIM:P_MO + (h + 1) * V_DIM]
        out = _head_norm(hid) * mng_ref[:, vs] * _sigmoid(gate)
        oml_ref[:, vs] = out.astype(BF16)
    y_ml = _dot(oml_ref[...], wpm_ref[...])

    while side_work:
        side()
    gate_a, gate_b = merge_gates
    mix = _dot((gate_a * y_ret + gate_b * y_ml).astype(BF16), wo_ref[...])
    return _layer_norm(alpha * x + mix, lng_ref[...], lnb_ref[...])


def _mixer_kernel(x_ref, xn_ref, cos_ref, sin_ref, wmain_ref, wif_ref, bif_ref,
                  bmerge_ref, convw_ref, convb_ref, rng_ref, mng_ref, wpr_ref, wpm_ref, wo_ref,
                  lng_ref, lnb_ref, dmask_ref, qdec_ref, kdec_ref,
                  o_ref,
                  r_ref, c_ref, n_ref, m_ref, conv_ref, oret_ref, oml_ref,
                  pa32_ref, pa16_ref, paif_ref, pb32_ref, pb16_ref, pbif_ref,
                  *, tm, d_model, alpha, chunk_decay):
    step_id = pl.program_id(1)
    pieces = functools.partial(_project_pieces, wmain_ref=wmain_ref, wif_ref=wif_ref,
                               bif_ref=bif_ref)
    buf_a = dict(p32_ref=pa32_ref, p16_ref=pa16_ref, pif_ref=paif_ref)
    buf_b = dict(p32_ref=pb32_ref, p16_ref=pb16_ref, pif_ref=pbif_ref)
    mix = functools.partial(
        _stage_mix, wmain_ref=wmain_ref, bmerge_ref=bmerge_ref, convw_ref=convw_ref,
        convb_ref=convb_ref, rng_ref=rng_ref, mng_ref=mng_ref, wpr_ref=wpr_ref, wpm_ref=wpm_ref,
        wo_ref=wo_ref, lng_ref=lng_ref, lnb_ref=lnb_ref, dmask_ref=dmask_ref, qdec_ref=qdec_ref,
        kdec_ref=kdec_ref, r_ref=r_ref, c_ref=c_ref, n_ref=n_ref, m_ref=m_ref, conv_ref=conv_ref,
        oret_ref=oret_ref, oml_ref=oml_ref, tm=tm, d_model=d_model, alpha=alpha,
        chunk_decay=chunk_decay)

    @pl.when(step_id == 0)
    def _():
        r_ref[...] = jnp.zeros_like(r_ref)
        c_ref[...] = jnp.zeros_like(c_ref)
        n_ref[...] = jnp.zeros_like(n_ref)
        m_ref[...] = jnp.zeros_like(m_ref)
        conv_ref[0:CONV_PAD, :] = jnp.zeros((CONV_PAD, 2 * _QK), F32)
        for piece in pieces(x_ref, **buf_a):
            piece()

    @pl.when(step_id % 2 == 0)
    def _():
        o_ref[0] = mix(x_ref, cos_ref[...], sin_ref[...], **buf_a, side_work=pieces(xn_ref, **buf_b))

    @pl.when(step_id % 2 == 1)
    def _():
        o_ref[0] = mix(x_ref, cos_ref[...], sin_ref[...], **buf_b, side_work=pieces(xn_ref, **buf_a))


def _ffn_kernel(x_ref, wgu_ref, wd_ref, lng_ref, lnb_ref, o_ref, *, alpha, d_ff):
    for i in range(x_ref.shape[0] // FFN_SUB):
        rows = slice(i * FFN_SUB, (i + 1) * FFN_SUB)
        x = x_ref[rows, :]
        xb = x.astype(BF16)
        gate = _dot(xb, wgu_ref[:, 0:d_ff])
        up = _dot(xb, wgu_ref[:, d_ff:2 * d_ff])
        hidden = (gate * _sigmoid(gate) * up).astype(BF16)
        y = alpha * x + _dot(hidden, wd_ref[...])
        o_ref[rows, :] = _layer_norm(y, lng_ref[...], lnb_ref[...])


def _layer_spec(arr, layer):
    tail = arr.shape[1:]
    zeros = (0,) * len(tail)
    return pl.BlockSpec((None,) + tail, lambda *_: (layer,) + zeros, pipeline_mode=pl.Buffered(1))


def _const_spec(arr):
    zeros = (0,) * arr.ndim
    return pl.BlockSpec(arr.shape, lambda *_: zeros, pipeline_mode=pl.Buffered(1))


def _mixer_call(x, cos, sin, layer_params, tables, *, layer, alpha, chunk_decay):
    batch, seq, d_model = x.shape
    tm = MIXER_TILE
    assert seq % tm == 0
    n_tiles = seq // tm
    kern = functools.partial(_mixer_kernel, tm=tm, d_model=d_model, alpha=alpha,
                             chunk_decay=chunk_decay)
    handoff = [pltpu.VMEM((tm, P_WIDTH), F32),
               pltpu.VMEM((tm, 2 * _V), BF16),
               pltpu.VMEM((GATE_ROWS, tm), F32)]
    return pl.pallas_call(
        kern,
        grid=(batch, n_tiles),
        in_specs=[pl.BlockSpec((1, tm, d_model), lambda b, j: (b, j, 0)),
                  pl.BlockSpec((1, tm, d_model),
                               lambda b, j: (b, jnp.minimum(j + 1, n_tiles - 1), 0)),
                  pl.BlockSpec((tm, QK_DIM), lambda b, j: (j, 0)),
                  pl.BlockSpec((tm, QK_DIM), lambda b, j: (j, 0))]
                 + [_layer_spec(p, layer) for p in layer_params]
                 + [_const_spec(t) for t in tables],
        out_specs=pl.BlockSpec((1, tm, d_model), lambda b, j: (b, j, 0)),
        out_shape=jax.ShapeDtypeStruct(x.shape, x.dtype),
        scratch_shapes=[
            pltpu.VMEM((HEADS, QK_DIM, V_DIM), F32),
            pltpu.VMEM((HEADS, QK_DIM, V_DIM), F32),
            pltpu.VMEM((8, QK_DIM), F32),
            pltpu.VMEM((8, 128), F32),
            pltpu.VMEM((CONV_PAD + tm, 2 * _QK), F32),
            pltpu.VMEM((tm, _V), BF16),
            pltpu.VMEM((tm, _V), BF16),
        ] + handoff + handoff,
        compiler_params=pltpu.CompilerParams(
            dimension_semantics=("arbitrary", "arbitrary"), vmem_limit_bytes=VMEM_LIMIT),
        name="mixer",
    )(x, x, cos, sin, *layer_params, *tables)


def _ffn_call(x2d, layer_params, *, layer, alpha):
    tokens, d_model = x2d.shape
    tm = min(FFN_TILE, tokens)
    assert tokens % tm == 0 and tm % FFN_SUB == 0
    d_ff = layer_params[1].shape[1]
    return pl.pallas_call(
        functools.partial(_ffn_kernel, alpha=alpha, d_ff=d_ff),
        grid=(tokens // tm,),
        in_specs=[pl.BlockSpec((tm, d_model), lambda i: (i, 0))]
                 + [_layer_spec(p, layer) for p in layer_params],
        out_specs=pl.BlockSpec((tm, d_model), lambda i: (i, 0)),
        out_shape=jax.ShapeDtypeStruct(x2d.shape, x2d.dtype),
        compiler_params=pltpu.CompilerParams(
            dimension_semantics=("arbitrary",), vmem_limit_bytes=VMEM_LIMIT),
        name="ffn",
    )(x2d, *layer_params)


def _position_tables(seq):
    half = QK_DIM // 2
    inv_freq = ROPE_BASE ** (-jnp.arange(half, dtype=F32) / half)
    ang = jnp.arange(seq, dtype=jnp.int32).astype(F32)[:, None] * inv_freq[None, :]
    cos = jnp.cos(ang)
    sin = jnp.sin(ang)
    return jnp.concatenate([cos, cos], axis=-1), jnp.concatenate([-sin, sin], axis=-1)


def _decay_tables(chunk):
    scale = QK_DIM ** -0.5
    log_gamma = jnp.log(1.0 - jnp.power(2.0, -5.0 - jnp.arange(HEADS, dtype=F32)))
    idx = jnp.arange(chunk, dtype=F32)
    rel = idx[:, None] - idx[None, :]
    causal = rel >= 0
    dmask = jnp.where(causal[None],
                      jnp.exp(log_gamma[:, None, None] * jnp.where(causal, rel, 0.0)[None]), 0.0) * scale
    qdec = jnp.exp(log_gamma[:, None] * (idx + 1.0)[None, :])
    kdec = jnp.exp(log_gamma[:, None] * (chunk - 1 - idx)[None, :]) * scale
    qdec = jnp.broadcast_to(qdec[:, :, None], (HEADS, chunk, QK_DIM))
    kdec = jnp.broadcast_to(kdec[:, None, :], (HEADS, QK_DIM, chunk))
    log_gamma64 = np.log(1.0 - np.power(2.0, -5.0 - np.arange(HEADS, dtype=np.float64)))
    chunk_decay = tuple(float(v) for v in np.exp(log_gamma64 * chunk))
    return dmask, qdec, kdec, chunk_decay


def kernel(x, w_in, b_if, b_merge, conv_w, conv_b, ret_norm_g, mlstm_norm_g, w_proj_ret,
           w_proj_mlstm, w_out, ln1_g, ln1_b, w_gate_up, w_down, ln2_g, ln2_b):
    batch, seq, d_model = x.shape
    depth = w_in.shape[0]
    d_ff = w_down.shape[1]
    alpha = float((2 * depth) ** 0.25)
    assert seq % MIXER_TILE == 0 and d_model % 128 == 0

    cos, sin = _position_tables(seq)
    dmask, qdec, kdec, chunk_decay = _decay_tables(MIXER_TILE)

    off_i = OFF_MO + _V
    w_main = jnp.concatenate([w_in[:, :, :off_i], w_in[:, :, off_i + 2 * HEADS:]], axis=-1).astype(BF16)
    w_if = w_in[:, :, off_i:off_i + 2 * HEADS]
    zcol = jnp.zeros((depth, d_model, 8 - HEADS), w_if.dtype)
    ztail = jnp.zeros((depth, d_model, GATE_COLS - GATE_ROWS), w_if.dtype)
    w_if = jnp.concatenate([w_if[:, :, :HEADS], zcol, w_if[:, :, HEADS:], zcol, ztail], axis=-1).astype(BF16)
    zb = jnp.zeros((depth, 8 - HEADS), b_if.dtype)
    zbt = jnp.zeros((depth, GATE_COLS - GATE_ROWS), b_if.dtype)
    bif = jnp.concatenate([b_if[:, :HEADS], zb, b_if[:, HEADS:], zb, zbt], axis=1).astype(F32)
    bif = bif[:, None, :]

    def rows(v):
        return v.reshape(depth, 1, -1).astype(F32)

    mixer_params = (w_main, w_if, bif, rows(b_merge), conv_w.astype(F32), rows(conv_b),
                    rows(ret_norm_g), rows(mlstm_norm_g), w_proj_ret.astype(BF16),
                    w_proj_mlstm.astype(BF16), w_out.astype(BF16), rows(ln1_g), rows(ln1_b))
    ffn_params = (w_gate_up.astype(BF16), w_down.astype(BF16), rows(ln2_g), rows(ln2_b))
    assert w_gate_up.shape[2] == 2 * d_ff and d_ff % 128 == 0

    for l in range(depth):
        x = _mixer_call(x, cos, sin, mixer_params, (dmask, qdec, kdec), layer=l, alpha=alpha,
                        chunk_decay=chunk_decay)
        x2 = _ffn_call(x.reshape(batch * seq, d_model), ffn_params, layer=l, alpha=alpha)
        x = x2.reshape(batch, seq, d_model)
    return x
```

```python
import functools

import jax
import jax.numpy as jnp
import numpy as np
from jax import lax
from jax.experimental import pallas as pl
from jax.experimental.pallas import tpu as pltpu

F32 = jnp.float32
BF16 = jnp.bfloat16

HEADS = 4
QK_DIM = 128
V_DIM = 256
CONV_WIDTH = 4
ROPE_BASE = 10000.0
NORM_EPS = 1e-5
CONV_PAD = 8
GATE_ROWS = 16
GATE_COLS = 128

MIXER_TILE = 256
FFN_TILE = 1024
FFN_SUB = 256
VMEM_LIMIT = 56 * 1024 * 1024

_QK = HEADS * QK_DIM
_V = HEADS * V_DIM
OFF_RQ = 0
OFF_RK = OFF_RQ + _QK
OFF_RV = OFF_RK + _QK
OFF_RG = OFF_RV + _V
OFF_MQK = OFF_RG + _V
OFF_MV = OFF_MQK + 2 * _QK
OFF_MO = OFF_MV + _V
OFF_IF = OFF_MO + _V
P_Q = 0
P_K = P_Q + _QK
P_G = P_K + _QK
P_MQK = P_G + _V
P_MO = P_MQK + 2 * _QK
P_WIDTH = P_MO + _V


def _dot(a, b):
    return jnp.dot(a, b, preferred_element_type=F32)


def _sigmoid(x):
    return 1.0 / (1.0 + jnp.exp(-x))


def _layer_norm(y, g, b):
    mu = jnp.mean(y, axis=-1, keepdims=True)
    d = y - mu
    var = jnp.mean(d * d, axis=-1, keepdims=True)
    return d * lax.rsqrt(var + NORM_EPS) * g + b


def _head_norm(o):
    mu = jnp.mean(o, axis=-1, keepdims=True)
    d = o - mu
    var = jnp.mean(d * d, axis=-1, keepdims=True)
    return d * lax.rsqrt(var + NORM_EPS)


def _lane_scan(x, lane, op, fill, width):
    d = 1
    while d < width:
        x = op(x, jnp.where(lane >= d, pltpu.roll(x, d, 1), fill))
        d *= 2
    return x


def _project_pieces(x_ref, wmain_ref, wif_ref, bif_ref, p32_ref, p16_ref, pif_ref):
    xb = x_ref[0].astype(BF16)

    def f32_piece(dst, src, width):
        def run():
            p32_ref[:, dst:dst + width] = _dot(xb, wmain_ref[:, src:src + width])
        return run

    def bf16_piece(dst, src, width):
        def run():
            p16_ref[:, dst:dst + width] = _dot(xb, wmain_ref[:, src:src + width]).astype(BF16)
        return run

    def gate_rows():
        pre = _dot(xb, wif_ref[...]) + bif_ref[...]
        pif_ref[...] = pre.T[0:GATE_ROWS, :]

    return [f32_piece(P_Q, OFF_RQ, 2 * _QK),
            bf16_piece(0, OFF_RV, _V),
            f32_piece(P_G, OFF_RG, _V),
            f32_piece(P_MQK, OFF_MQK, 2 * _QK),
            bf16_piece(_V, OFF_MV, _V),
            f32_piece(P_MO, OFF_MO, _V),
            gate_rows]


def _stage_mix(x_ref, cos, sin, p32_ref, p16_ref, pif_ref, wgab_ref, bmerge_ref, convw_ref,
               convb_ref, rng_ref, mng_ref, wpr_ref, wpm_ref, wo_ref, lng_ref, lnb_ref,
               dmask_ref, qdec_ref, kdec_ref, r_ref, c_ref, n_ref, m_ref, conv_ref, oret_ref,
               oml_ref, *, tm, d_model, alpha, chunk_decay, side_work=()):
    side_work = list(side_work)

    def side():
        if side_work:
            side_work.pop(0)()

    lane = lax.broadcasted_iota(jnp.int32, (8, tm), 1)
    row_i = lax.broadcasted_iota(jnp.int32, (tm, tm), 0)
    col_i = lax.broadcasted_iota(jnp.int32, (tm, tm), 1)
    causal = col_i <= row_i
    neg_inf = jnp.float32(-jnp.inf)

    m_prev = m_ref[:, 0:1]
    i_pre = pif_ref[0:8, :]
    f_pre = pif_ref[8:16, :]
    log_f = jnp.minimum(f_pre, 0.0) - jnp.log1p(jnp.exp(-jnp.abs(f_pre)))
    b = _lane_scan(log_f, lane, jnp.add, 0.0, tm)
    a = i_pre - b
    cm = _lane_scan(a, lane, jnp.maximum, neg_inf, tm)
    g = jnp.maximum(cm, m_prev)
    inter = jnp.exp(m_prev - g)
    exp_nm = jnp.exp(-(b + g))
    b_end = b[:, tm - 1:tm]
    cm_end = cm[:, tm - 1:tm]
    m_loc = b_end + cm_end
    w_end = jnp.exp(a - cm_end)
    m_new = jnp.maximum(b_end + m_prev, m_loc)
    a_c = jnp.broadcast_to(jnp.exp(b_end + m_prev - m_new), (8, V_DIM))
    c_c = jnp.broadcast_to(jnp.exp(m_loc - m_new), (8, V_DIM))
    m_ref[...] = jnp.broadcast_to(m_new, m_ref.shape)
    pad_rows = jnp.zeros((128 - 32, tm), F32)
    zt = jnp.concatenate([g, inter, exp_nm, w_end, pad_rows], axis=0).T

    for h in range(HEADS):
        side()
        vs = slice(h * V_DIM, (h + 1) * V_DIM)
        qc = p32_ref[:, P_Q + h * QK_DIM:P_Q + (h + 1) * QK_DIM]
        kc = p32_ref[:, P_K + h * QK_DIM:P_K + (h + 1) * QK_DIM]
        qc = qc * cos + pltpu.roll(qc, QK_DIM // 2, 1) * sin
        kc = kc * cos + pltpu.roll(kc, QK_DIM // 2, 1) * sin
        vc = p16_ref[:, vs]
        kt = kc.T
        s = _dot(qc.astype(BF16), kt.astype(BF16)) * dmask_ref[h]
        r_state = r_ref[h]
        o = (_dot(s.astype(BF16), vc)
             + _dot((qc * qdec_ref[h]).astype(BF16), r_state.astype(BF16)))
        r_ref[h] = r_state * chunk_decay[h] + _dot((kt * kdec_ref[h]).astype(BF16), vc)
        gate = p32_ref[:, P_G + h * V_DIM:P_G + (h + 1) * V_DIM]
        out = _head_norm(o) * rng_ref[:, vs] * (gate * _sigmoid(gate))
        oret_ref[:, vs] = out.astype(BF16)
    y_ret = _dot(oret_ref[...], wpr_ref[...])

    mqk = p32_ref[:, P_MQK:P_MQK + 2 * _QK]
    conv_ref[CONV_PAD:CONV_PAD + tm, :] = mqk
    y = convb_ref[...] + convw_ref[CONV_WIDTH - 1:CONV_WIDTH, :] * mqk
    for j in range(CONV_WIDTH - 1):
        lo = CONV_PAD - (CONV_WIDTH - 1) + j
        y = y + convw_ref[j:j + 1, :] * conv_ref[lo:lo + tm, :]
    conv_ref[0:CONV_PAD, :] = mqk[tm - CONV_PAD:tm, :]
    qk_c = y * _sigmoid(y)

    x = x_ref[0]
    xb = x.astype(BF16)

    def merge_gate(k):
        pre = _dot(xb, wgab_ref[:, k * d_model:(k + 1) * d_model])
        return _sigmoid(pre + bmerge_ref[:, k * d_model:(k + 1) * d_model])

    merge_gates = []
    n_prev = n_ref[...]
    for h in range(HEADS):
        side()
        if h >= 2:
            merge_gates.append(merge_gate(h - 2))
        vs = slice(h * V_DIM, (h + 1) * V_DIM)
        qc = qk_c[:, h * QK_DIM:(h + 1) * QK_DIM]
        kc = qk_c[:, _QK + h * QK_DIM:_QK + (h + 1) * QK_DIM] * (QK_DIM ** -0.5)
        vc = p16_ref[:, _V + h * V_DIM:_V + (h + 1) * V_DIM]
        g_col = zt[:, h:h + 1]
        inter_col = zt[:, 8 + h:9 + h]
        expnm_col = zt[:, 16 + h:17 + h]
        wend_col = zt[:, 24 + h:25 + h]
        dmat = jnp.exp(jnp.where(causal, a[h:h + 1, :] - g_col, neg_inf))
        qcb = qc.astype(BF16)
        kt = kc.T
        qk = _dot(qcb, kt.astype(BF16)) * dmat
        c_state = c_ref[h]
        num = _dot(qk.astype(BF16), vc) + _dot(qcb, c_state.astype(BF16)) * inter_col
        den = (jnp.sum(qk, axis=-1, keepdims=True)
               + jnp.sum(qc * n_prev[h:h + 1, :], axis=-1, keepdims=True) * inter_col)
        denom = jnp.maximum(jnp.abs(den), expnm_col)
        hid = num * (1.0 / denom)
        kw = kc * wend_col
        kv = _dot((kt * w_end[h:h + 1, :]).astype(BF16), vc)
        ac_h = a_c[h:h + 1, :]
        cc_h = c_c[h:h + 1, :]
        c_ref[h] = ac_h * c_state + cc_h * kv
        n_ref[h:h + 1, :] = (ac_h[:, 0:QK_DIM] * n_prev[h:h + 1, :]
                             + cc_h[:, 0:QK_DIM] * jnp.sum(kw, axis=0, keepdims=True))
        gate = p32_ref[:, P_MO + h * V_DIM:P_MO + (h + 1) * V_DIM]
        out = _head_norm(hid) * mng_ref[:, vs] * _sigmoid(gate)
        oml_ref[:, vs] = out.astype(BF16)
    y_ml = _dot(oml_ref[...], wpm_ref[...])

    while side_work:
        side()
    gate_a, gate_b = merge_gates
    mix = _dot((gate_a * y_ret + gate_b * y_ml).astype(BF16), wo_ref[...])
    return _layer_norm(alpha * x + mix, lng_ref[...], lnb_ref[...])


def _mixer_kernel(x_ref, xn_ref, cos_ref, sin_ref, wmain_ref, wgab_ref, wif_ref, bif_ref,
                  bmerge_ref, convw_ref, convb_ref, rng_ref, mng_ref, wpr_ref, wpm_ref, wo_ref,
                  lng_ref, lnb_ref, dmask_ref, qdec_ref, kdec_ref,
                  o_ref,
                  r_ref, c_ref, n_ref, m_ref, conv_ref, oret_ref, oml_ref,
                  pa32_ref, pa16_ref, paif_ref, pb32_ref, pb16_ref, pbif_ref,
                  *, tm, d_model, alpha, chunk_decay):
    step_id = pl.program_id(1)
    pieces = functools.partial(_project_pieces, wmain_ref=wmain_ref, wif_ref=wif_ref,
                               bif_ref=bif_ref)
    buf_a = dict(p32_ref=pa32_ref, p16_ref=pa16_ref, pif_ref=paif_ref)
    buf_b = dict(p32_ref=pb32_ref, p16_ref=pb16_ref, pif_ref=pbif_ref)
    mix = functools.partial(
        _stage_mix, wgab_ref=wgab_ref, bmerge_ref=bmerge_ref, convw_ref=convw_ref,
        convb_ref=convb_ref, rng_ref=rng_ref, mng_ref=mng_ref, wpr_ref=wpr_ref, wpm_ref=wpm_ref,
        wo_ref=wo_ref, lng_ref=lng_ref, lnb_ref=lnb_ref, dmask_ref=dmask_ref, qdec_ref=qdec_ref,
        kdec_ref=kdec_ref, r_ref=r_ref, c_ref=c_ref, n_ref=n_ref, m_ref=m_ref, conv_ref=conv_ref,
        oret_ref=oret_ref, oml_ref=oml_ref, tm=tm, d_model=d_model, alpha=alpha,
        chunk_decay=chunk_decay)

    @pl.when(step_id == 0)
    def _():
        r_ref[...] = jnp.zeros_like(r_ref)
        c_ref[...] = jnp.zeros_like(c_ref)
        n_ref[...] = jnp.zeros_like(n_ref)
        m_ref[...] = jnp.zeros_like(m_ref)
        conv_ref[0:CONV_PAD, :] = jnp.zeros((CONV_PAD, 2 * _QK), F32)
        for piece in pieces(x_ref, **buf_a):
            piece()

    @pl.when(step_id % 2 == 0)
    def _():
        o_ref[0] = mix(x_ref, cos_ref[...], sin_ref[...], **buf_a, side_work=pieces(xn_ref, **buf_b))

    @pl.when(step_id % 2 == 1)
    def _():
        o_ref[0] = mix(x_ref, cos_ref[...], sin_ref[...], **buf_b, side_work=pieces(xn_ref, **buf_a))


def _ffn_kernel(x_ref, wgu_ref, wd_ref, lng_ref, lnb_ref, o_ref, *, alpha, d_ff):
    for i in range(x_ref.shape[0] // FFN_SUB):
        rows = slice(i * FFN_SUB, (i + 1) * FFN_SUB)
        x = x_ref[rows, :]
        xb = x.astype(BF16)
        gate = _dot(xb, wgu_ref[:, 0:d_ff])
        up = _dot(xb, wgu_ref[:, d_ff:2 * d_ff])
        hidden = (gate * _sigmoid(gate) * up).astype(BF16)
        y = alpha * x + _dot(hidden, wd_ref[...])
        o_ref[rows, :] = _layer_norm(y, lng_ref[...], lnb_ref[...])


def _layer_spec(arr, layer):
    tail = arr.shape[1:]
    zeros = (0,) * len(tail)
    return pl.BlockSpec((None,) + tail, lambda *_: (layer,) + zeros, pipeline_mode=pl.Buffered(1))


def _const_spec(arr):
    zeros = (0,) * arr.ndim
    return pl.BlockSpec(arr.shape, lambda *_: zeros, pipeline_mode=pl.Buffered(1))


def _mixer_call(x, cos, sin, layer_params, tables, *, layer, alpha, chunk_decay):
    batch, seq, d_model = x.shape
    tm = MIXER_TILE
    assert seq % tm == 0
    n_tiles = seq // tm
    kern = functools.partial(_mixer_kernel, tm=tm, d_model=d_model, alpha=alpha,
                             chunk_decay=chunk_decay)
    handoff = [pltpu.VMEM((tm, P_WIDTH), F32),
               pltpu.VMEM((tm, 2 * _V), BF16),
               pltpu.VMEM((GATE_ROWS, tm), F32)]
    return pl.pallas_call(
        kern,
        grid=(batch, n_tiles),
        in_specs=[pl.BlockSpec((1, tm, d_model), lambda b, j: (b, j, 0)),
                  pl.BlockSpec((1, tm, d_model),
                               lambda b, j: (b, jnp.minimum(j + 1, n_tiles - 1), 0)),
                  pl.BlockSpec((tm, QK_DIM), lambda b, j: (j, 0)),
                  pl.BlockSpec((tm, QK_DIM), lambda b, j: (j, 0))]
                 + [_layer_spec(p, layer) for p in layer_params]
                 + [_const_spec(t) for t in tables],
        out_specs=pl.BlockSpec((1, tm, d_model), lambda b, j: (b, j, 0)),
        out_shape=jax.ShapeDtypeStruct(x.shape, x.dtype),
        scratch_shapes=[
            pltpu.VMEM((HEADS, QK_DIM, V_DIM), F32),
            pltpu.VMEM((HEADS, QK_DIM, V_DIM), F32),
            pltpu.VMEM((8, QK_DIM), F32),
            pltpu.VMEM((8, 128), F32),
            pltpu.VMEM((CONV_PAD + tm, 2 * _QK), F32),
            pltpu.VMEM((tm, _V), BF16),
            pltpu.VMEM((tm, _V), BF16),
        ] + handoff + handoff,
        compiler_params=pltpu.CompilerParams(
            dimension_semantics=("arbitrary", "arbitrary"), vmem_limit_bytes=VMEM_LIMIT),
        name="mixer",
    )(x, x, cos, sin, *layer_params, *tables)


def _ffn_call(x2d, layer_params, *, layer, alpha):
    tokens, d_model = x2d.shape
    tm = min(FFN_TILE, tokens)
    assert tokens % tm == 0 and tm % FFN_SUB == 0
    d_ff = layer_params[1].shape[1]
    return pl.pallas_call(
        functools.partial(_ffn_kernel, alpha=alpha, d_ff=d_ff),
        grid=(tokens // tm,),
        in_specs=[pl.BlockSpec((tm, d_model), lambda i: (i, 0))]
                 + [_layer_spec(p, layer) for p in layer_params],
        out_specs=pl.BlockSpec((tm, d_model), lambda i: (i, 0)),
        out_shape=jax.ShapeDtypeStruct(x2d.shape, x2d.dtype),
        compiler_params=pltpu.CompilerParams(
            dimension_semantics=("arbitrary",), vmem_limit_bytes=VMEM_LIMIT),
        name="ffn",
    )(x2d, *layer_params)


def _position_tables(seq):
    half = QK_DIM // 2
    inv_freq = ROPE_BASE ** (-jnp.arange(half, dtype=F32) / half)
    ang = jnp.arange(seq, dtype=jnp.int32).astype(F32)[:, None] * inv_freq[None, :]
    cos = jnp.cos(ang)
    sin = jnp.sin(ang)
    return jnp.concatenate([cos, cos], axis=-1), jnp.concatenate([-sin, sin], axis=-1)


def _decay_tables(chunk):
    scale = QK_DIM ** -0.5
    log_gamma = jnp.log(1.0 - jnp.power(2.0, -5.0 - jnp.arange(HEADS, dtype=F32)))
    idx = jnp.arange(chunk, dtype=F32)
    rel = idx[:, None] - idx[None, :]
    causal = rel >= 0
    dmask = jnp.where(causal[None],
                      jnp.exp(log_gamma[:, None, None] * jnp.where(causal, rel, 0.0)[None]), 0.0) * scale
    qdec = jnp.exp(log_gamma[:, None] * (idx + 1.0)[None, :])
    kdec = jnp.exp(log_gamma[:, None] * (chunk - 1 - idx)[None, :]) * scale
    qdec = jnp.broadcast_to(qdec[:, :, None], (HEADS, chunk, QK_DIM))
    kdec = jnp.broadcast_to(kdec[:, None, :], (HEADS, QK_DIM, chunk))
    log_gamma64 = np.log(1.0 - np.power(2.0, -5.0 - np.arange(HEADS, dtype=np.float64)))
    chunk_decay = tuple(float(v) for v in np.exp(log_gamma64 * chunk))
    return dmask, qdec, kdec, chunk_decay


def kernel(x, w_in, b_if, b_merge, conv_w, conv_b, ret_norm_g, mlstm_norm_g, w_proj_ret,
           w_proj_mlstm, w_out, ln1_g, ln1_b, w_gate_up, w_down, ln2_g, ln2_b):
    batch, seq, d_model = x.shape
    depth = w_in.shape[0]
    d_ff = w_down.shape[1]
    alpha = float((2 * depth) ** 0.25)
    assert seq % MIXER_TILE == 0 and d_model % 128 == 0

    cos, sin = _position_tables(seq)
    dmask, qdec, kdec, chunk_decay = _decay_tables(MIXER_TILE)

    off_i = OFF_IF
    w_main = w_in[:, :, :off_i].astype(BF16)
    w_gab = w_in[:, :, off_i + 2 * HEADS:].astype(BF16)
    w_if = w_in[:, :, off_i:off_i + 2 * HEADS]
    zcol = jnp.zeros((depth, d_model, 8 - HEADS), w_if.dtype)
    ztail = jnp.zeros((depth, d_model, GATE_COLS - GATE_ROWS), w_if.dtype)
    w_if = jnp.concatenate([w_if[:, :, :HEADS], zcol, w_if[:, :, HEADS:], zcol, ztail], axis=-1).astype(BF16)
    zb = jnp.zeros((depth, 8 - HEADS), b_if.dtype)
    zbt = jnp.zeros((depth, GATE_COLS - GATE_ROWS), b_if.dtype)
    bif = jnp.concatenate([b_if[:, :HEADS], zb, b_if[:, HEADS:], zb, zbt], axis=1).astype(F32)
    bif = bif[:, None, :]

    def rows(v):
        return v.reshape(depth, 1, -1).astype(F32)

    mixer_params = (w_main, w_gab, w_if, bif, rows(b_merge), conv_w.astype(F32), rows(conv_b),
                    rows(ret_norm_g), rows(mlstm_norm_g), w_proj_ret.astype(BF16),
                    w_proj_mlstm.astype(BF16), w_out.astype(BF16), rows(ln1_g), rows(ln1_b))
    ffn_params = (w_gate_up.astype(BF16), w_down.astype(BF16), rows(ln2_g), rows(ln2_b))
    assert w_gate_up.shape[2] == 2 * d_ff and d_ff % 128 == 0

    for l in range(depth):
        x = _mixer_call(x, cos, sin, mixer_params, (dmask, qdec, kdec), layer=l, alpha=alpha,
                        chunk_decay=chunk_decay)
        x2 = _ffn_call(x.reshape(batch * seq, d_model), ffn_params, layer=l, alpha=alpha)
        x = x2.reshape(batch, seq, d_model)
    return x
```

```python
import functools

import jax
import jax.numpy as jnp
import numpy as np
from jax import lax
from jax.experimental import pallas as pl
from jax.experimental.pallas import tpu as pltpu

F32 = jnp.float32
BF16 = jnp.bfloat16

HEADS = 4
QK_DIM = 128
V_DIM = 256
CONV_WIDTH = 4
ROPE_BASE = 10000.0
NORM_EPS = 1e-5
CONV_PAD = 8
GATE_ROWS = 16
GATE_COLS = 128
LANE_PAD = 128

MIXER_TILE = 256
FFN_TILE = 1024
FFN_SUB = 256
VMEM_LIMIT = 56 * 1024 * 1024

_QK = HEADS * QK_DIM
_V = HEADS * V_DIM
OFF_RQ = 0
OFF_RK = OFF_RQ + _QK
OFF_RV = OFF_RK + _QK
OFF_RG = OFF_RV + _V
OFF_MQK = OFF_RG + _V
OFF_MV = OFF_MQK + 2 * _QK
OFF_MO = OFF_MV + _V
OFF_IF = OFF_MO + _V
P_Q = 0
P_K = P_Q + _QK
P_G = P_K + _QK
P_MQK = P_G + _V
P_MO = P_MQK + 2 * _QK
P_WIDTH = P_MO + _V


def _dot(a, b):
    return jnp.dot(a, b, preferred_element_type=F32)


def _sigmoid(x):
    return 1.0 / (1.0 + jnp.exp(-x))


def _layer_norm(y, g, b):
    mu = jnp.mean(y, axis=-1, keepdims=True)
    d = y - mu
    var = jnp.mean(d * d, axis=-1, keepdims=True)
    return d * lax.rsqrt(var + NORM_EPS) * g + b


def _head_norm(o):
    mu = jnp.mean(o, axis=-1, keepdims=True)
    d = o - mu
    var = jnp.mean(d * d, axis=-1, keepdims=True)
    return d * lax.rsqrt(var + NORM_EPS)


def _lane_scan(x, lane, op, fill, width):
    d = 1
    while d < width:
        x = op(x, jnp.where(lane >= d, pltpu.roll(x, d, 1), fill))
        d *= 2
    return x


def _project_pieces(x_ref, wmain_ref, wif_ref, bif_ref, p32_ref, p16_ref, pif_ref):
    xb = x_ref[0].astype(BF16)

    def f32_piece(dst, src, width):
        def run():
            p32_ref[:, dst:dst + width] = _dot(xb, wmain_ref[:, src:src + width])
        return run

    def bf16_piece(dst, src, width):
        def run():
            p16_ref[:, dst:dst + width] = _dot(xb, wmain_ref[:, src:src + width]).astype(BF16)
        return run

    def gate_rows():
        pre = _dot(xb, wif_ref[...]) + bif_ref[...]
        pif_ref[...] = pre.T[0:GATE_ROWS, :]

    return [f32_piece(P_Q, OFF_RQ, 2 * _QK),
            bf16_piece(0, OFF_RV, _V),
            f32_piece(P_G, OFF_RG, _V),
            f32_piece(P_MQK, OFF_MQK, 2 * _QK),
            bf16_piece(_V, OFF_MV, _V),
            f32_piece(P_MO, OFF_MO, _V),
            gate_rows]


def _stage_mix(x_ref, cos, sin, p32_ref, p16_ref, pif_ref, wgab_ref, bmerge_ref, convw_ref,
               convb_ref, rng_ref, mng_ref, wpr_ref, wpm_ref, wo_ref, lng_ref, lnb_ref,
               dmask_ref, qdec_ref, kdec_ref, r_ref, c_ref, n_ref, m_ref, conv_ref, oret_ref,
               oml_ref, *, tm, d_model, alpha, chunk_decay, side_work=()):
    side_work = list(side_work)

    def side():
        if side_work:
            side_work.pop(0)()

    lane = lax.broadcasted_iota(jnp.int32, (8, tm), 1)
    row_i = lax.broadcasted_iota(jnp.int32, (tm, tm), 0)
    col_i = lax.broadcasted_iota(jnp.int32, (tm, tm), 1)
    causal = col_i <= row_i
    neg_inf = jnp.float32(-jnp.inf)

    m_prev = m_ref[:, 0:1]
    i_pre = pif_ref[0:8, :]
    f_pre = pif_ref[8:16, :]
    log_f = jnp.minimum(f_pre, 0.0) - jnp.log1p(jnp.exp(-jnp.abs(f_pre)))
    b = _lane_scan(log_f, lane, jnp.add, 0.0, tm)
    a = i_pre - b
    cm = _lane_scan(a, lane, jnp.maximum, neg_inf, tm)
    g = jnp.maximum(cm, m_prev)
    inter = jnp.exp(m_prev - g)
    exp_nm = jnp.exp(-(b + g))
    b_end = b[:, tm - 1:tm]
    cm_end = cm[:, tm - 1:tm]
    m_loc = b_end + cm_end
    w_end = jnp.exp(a - cm_end)
    m_new = jnp.maximum(b_end + m_prev, m_loc)
    a_c = jnp.broadcast_to(jnp.exp(b_end + m_prev - m_new), (8, V_DIM))
    c_c = jnp.broadcast_to(jnp.exp(m_loc - m_new), (8, V_DIM))
    m_ref[...] = jnp.broadcast_to(m_new, m_ref.shape)
    pad_rows = jnp.zeros((128 - 32, tm), F32)
    zt = jnp.concatenate([g, inter, exp_nm, w_end, pad_rows], axis=0).T

    for h in range(HEADS):
        side()
        vs = slice(h * V_DIM, (h + 1) * V_DIM)
        qc = p32_ref[:, P_Q + h * QK_DIM:P_Q + (h + 1) * QK_DIM]
        kc = p32_ref[:, P_K + h * QK_DIM:P_K + (h + 1) * QK_DIM]
        qc = qc * cos + pltpu.roll(qc, QK_DIM // 2, 1) * sin
        kc = kc * cos + pltpu.roll(kc, QK_DIM // 2, 1) * sin
        vc = p16_ref[:, vs]
        kt = kc.T
        s = _dot(qc.astype(BF16), kt.astype(BF16)) * dmask_ref[h]
        r_state = r_ref[h]
        o = (_dot(s.astype(BF16), vc)
             + _dot((qc * qdec_ref[h]).astype(BF16), r_state.astype(BF16)))
        r_ref[h] = r_state * chunk_decay[h] + _dot((kt * kdec_ref[h]).astype(BF16), vc)
        gate = p32_ref[:, P_G + h * V_DIM:P_G + (h + 1) * V_DIM]
        out = _head_norm(o) * rng_ref[:, vs] * (gate * _sigmoid(gate))
        oret_ref[:, vs] = out.astype(BF16)
    y_ret = _dot(oret_ref[...], wpr_ref[:, 0:d_model])

    mqk = p32_ref[:, P_MQK:P_MQK + 2 * _QK]
    conv_ref[CONV_PAD:CONV_PAD + tm, :] = mqk
    y = convb_ref[...] + convw_ref[CONV_WIDTH - 1:CONV_WIDTH, :] * mqk
    for j in range(CONV_WIDTH - 1):
        lo = CONV_PAD - (CONV_WIDTH - 1) + j
        y = y + convw_ref[j:j + 1, :] * conv_ref[lo:lo + tm, :]
    conv_ref[0:CONV_PAD, :] = mqk[tm - CONV_PAD:tm, :]
    qk_c = y * _sigmoid(y)

    x = x_ref[0]
    xb = x.astype(BF16)

    def merge_gate(k):
        pre = _dot(xb, wgab_ref[:, k * d_model:(k + 1) * d_model])
        return _sigmoid(pre + bmerge_ref[:, k * d_model:(k + 1) * d_model])

    merge_gates = []
    n_prev = n_ref[...]
    for h in range(HEADS):
        side()
        if h >= 2:
            merge_gates.append(merge_gate(h - 2))
        vs = slice(h * V_DIM, (h + 1) * V_DIM)
        qc = qk_c[:, h * QK_DIM:(h + 1) * QK_DIM]
        kc = qk_c[:, _QK + h * QK_DIM:_QK + (h + 1) * QK_DIM] * (QK_DIM ** -0.5)
        vc = p16_ref[:, _V + h * V_DIM:_V + (h + 1) * V_DIM]
        g_col = zt[:, h:h + 1]
        inter_col = zt[:, 8 + h:9 + h]
        expnm_col = zt[:, 16 + h:17 + h]
        wend_col = zt[:, 24 + h:25 + h]
        dmat = jnp.exp(jnp.where(causal, a[h:h + 1, :] - g_col, neg_inf))
        qcb = qc.astype(BF16)
        kt = kc.T
        qk = _dot(qcb, kt.astype(BF16)) * dmat
        c_state = c_ref[h]
        num = _dot(qk.astype(BF16), vc) + _dot(qcb, c_state.astype(BF16)) * inter_col
        den = (jnp.sum(qk, axis=-1, keepdims=True)
               + jnp.sum(qc * n_prev[h:h + 1, :], axis=-1, keepdims=True) * inter_col)
        denom = jnp.maximum(jnp.abs(den), expnm_col)
        hid = num * (1.0 / denom)
        kw = kc * wend_col
        kv = _dot((kt * w_end[h:h + 1, :]).astype(BF16), vc)
        ac_h = a_c[h:h + 1, :]
        cc_h = c_c[h:h + 1, :]
        c_ref[h] = ac_h * c_state + cc_h * kv
        n_ref[h:h + 1, :] = (ac_h[:, 0:QK_DIM] * n_prev[h:h + 1, :]
                             + cc_h[:, 0:QK_DIM] * jnp.sum(kw, axis=0, keepdims=True))
        gate = p32_ref[:, P_MO + h * V_DIM:P_MO + (h + 1) * V_DIM]
        out = _head_norm(hid) * mng_ref[:, vs] * _sigmoid(gate)
        oml_ref[:, vs] = out.astype(BF16)
    y_ml = _dot(oml_ref[...], wpm_ref[:, 0:d_model])

    while side_work:
        side()
    gate_a, gate_b = merge_gates
    mix = _dot((gate_a * y_ret + gate_b * y_ml).astype(BF16), wo_ref[:, 0:d_model])
    return _layer_norm(alpha * x + mix, lng_ref[...], lnb_ref[...])


def _mixer_kernel(x_ref, xn_ref, cos_ref, sin_ref, wmain_ref, wgab_ref, wif_ref, bif_ref,
                  bmerge_ref, convw_ref, convb_ref, rng_ref, mng_ref, wpr_ref, wpm_ref, wo_ref,
                  lng_ref, lnb_ref, dmask_ref, qdec_ref, kdec_ref,
                  o_ref,
                  r_ref, c_ref, n_ref, m_ref, conv_ref, oret_ref, oml_ref,
                  pa32_ref, pa16_ref, paif_ref, pb32_ref, pb16_ref, pbif_ref,
                  *, tm, d_model, alpha, chunk_decay):
    step_id = pl.program_id(1)
    pieces = functools.partial(_project_pieces, wmain_ref=wmain_ref, wif_ref=wif_ref,
                               bif_ref=bif_ref)
    buf_a = dict(p32_ref=pa32_ref, p16_ref=pa16_ref, pif_ref=paif_ref)
    buf_b = dict(p32_ref=pb32_ref, p16_ref=pb16_ref, pif_ref=pbif_ref)
    mix = functools.partial(
        _stage_mix, wgab_ref=wgab_ref, bmerge_ref=bmerge_ref, convw_ref=convw_ref,
        convb_ref=convb_ref, rng_ref=rng_ref, mng_ref=mng_ref, wpr_ref=wpr_ref, wpm_ref=wpm_ref,
        wo_ref=wo_ref, lng_ref=lng_ref, lnb_ref=lnb_ref, dmask_ref=dmask_ref, qdec_ref=qdec_ref,
        kdec_ref=kdec_ref, r_ref=r_ref, c_ref=c_ref, n_ref=n_ref, m_ref=m_ref, conv_ref=conv_ref,
        oret_ref=oret_ref, oml_ref=oml_ref, tm=tm, d_model=d_model, alpha=alpha,
        chunk_decay=chunk_decay)

    @pl.when(step_id == 0)
    def _():
        r_ref[...] = jnp.zeros_like(r_ref)
        c_ref[...] = jnp.zeros_like(c_ref)
        n_ref[...] = jnp.zeros_like(n_ref)
        m_ref[...] = jnp.zeros_like(m_ref)
        conv_ref[0:CONV_PAD, :] = jnp.zeros((CONV_PAD, 2 * _QK), F32)
        for piece in pieces(x_ref, **buf_a):
            piece()

    @pl.when(step_id % 2 == 0)
    def _():
        o_ref[0] = mix(x_ref, cos_ref[...], sin_ref[...], **buf_a, side_work=pieces(xn_ref, **buf_b))

    @pl.when(step_id % 2 == 1)
    def _():
        o_ref[0] = mix(x_ref, cos_ref[...], sin_ref[...], **buf_b, side_work=pieces(xn_ref, **buf_a))


def _ffn_kernel(x_ref, wgu_ref, wd_ref, lng_ref, lnb_ref, o_ref, *, alpha, d_ff):
    for i in range(x_ref.shape[0] // FFN_SUB):
        rows = slice(i * FFN_SUB, (i + 1) * FFN_SUB)
        x = x_ref[rows, :]
        xb = x.astype(BF16)
        gate = _dot(xb, wgu_ref[:, 0:d_ff])
        up = _dot(xb, wgu_ref[:, d_ff:2 * d_ff])
        hidden = (gate * _sigmoid(gate) * up).astype(BF16)
        y = alpha * x + _dot(hidden, wd_ref[...])
        o_ref[rows, :] = _layer_norm(y, lng_ref[...], lnb_ref[...])


def _layer_spec(arr, layer):
    tail = arr.shape[1:]
    zeros = (0,) * len(tail)
    return pl.BlockSpec((None,) + tail, lambda *_: (layer,) + zeros, pipeline_mode=pl.Buffered(1))


def _const_spec(arr):
    zeros = (0,) * arr.ndim
    return pl.BlockSpec(arr.shape, lambda *_: zeros, pipeline_mode=pl.Buffered(1))


def _mixer_call(x, cos, sin, layer_params, tables, *, layer, alpha, chunk_decay):
    batch, seq, d_model = x.shape
    tm = MIXER_TILE
    assert seq % tm == 0
    n_tiles = seq // tm
    kern = functools.partial(_mixer_kernel, tm=tm, d_model=d_model, alpha=alpha,
                             chunk_decay=chunk_decay)
    handoff = [pltpu.VMEM((tm, P_WIDTH), F32),
               pltpu.VMEM((tm, 2 * _V), BF16),
               pltpu.VMEM((GATE_ROWS, tm), F32)]
    return pl.pallas_call(
        kern,
        grid=(batch, n_tiles),
        in_specs=[pl.BlockSpec((1, tm, d_model), lambda b, j: (b, j, 0)),
                  pl.BlockSpec((1, tm, d_model),
                               lambda b, j: (b, jnp.minimum(j + 1, n_tiles - 1), 0)),
                  pl.BlockSpec((tm, QK_DIM), lambda b, j: (j, 0)),
                  pl.BlockSpec((tm, QK_DIM), lambda b, j: (j, 0))]
                 + [_layer_spec(p, layer) for p in layer_params]
                 + [_const_spec(t) for t in tables],
        out_specs=pl.BlockSpec((1, tm, d_model), lambda b, j: (b, j, 0)),
        out_shape=jax.ShapeDtypeStruct(x.shape, x.dtype),
        scratch_shapes=[
            pltpu.VMEM((HEADS, QK_DIM, V_DIM), F32),
            pltpu.VMEM((HEADS, QK_DIM, V_DIM), F32),
            pltpu.VMEM((8, QK_DIM), F32),
            pltpu.VMEM((8, 128), F32),
            pltpu.VMEM((CONV_PAD + tm, 2 * _QK), F32),
            pltpu.VMEM((tm, _V), BF16),
            pltpu.VMEM((tm, _V), BF16),
        ] + handoff + handoff,
        compiler_params=pltpu.CompilerParams(
            dimension_semantics=("arbitrary", "arbitrary"), vmem_limit_bytes=VMEM_LIMIT),
        name="mixer",
    )(x, x, cos, sin, *layer_params, *tables)


def _ffn_call(x2d, layer_params, *, layer, alpha):
    tokens, d_model = x2d.shape
    tm = min(FFN_TILE, tokens)
    assert tokens % tm == 0 and tm % FFN_SUB == 0
    d_ff = layer_params[1].shape[1]
    return pl.pallas_call(
        functools.partial(_ffn_kernel, alpha=alpha, d_ff=d_ff),
        grid=(tokens // tm,),
        in_specs=[pl.BlockSpec((tm, d_model), lambda i: (i, 0))]
                 + [_layer_spec(p, layer) for p in layer_params],
        out_specs=pl.BlockSpec((tm, d_model), lambda i: (i, 0)),
        out_shape=jax.ShapeDtypeStruct(x2d.shape, x2d.dtype),
        compiler_params=pltpu.CompilerParams(
            dimension_semantics=("arbitrary",), vmem_limit_bytes=VMEM_LIMIT),
        name="ffn",
    )(x2d, *layer_params)


def _position_tables(seq):
    half = QK_DIM // 2
    inv_freq = ROPE_BASE ** (-jnp.arange(half, dtype=F32) / half)
    ang = jnp.arange(seq, dtype=jnp.int32).astype(F32)[:, None] * inv_freq[None, :]
    cos = jnp.cos(ang)
    sin = jnp.sin(ang)
    return jnp.concatenate([cos, cos], axis=-1), jnp.concatenate([-sin, sin], axis=-1)


def _decay_tables(chunk):
    scale = QK_DIM ** -0.5
    log_gamma = jnp.log(1.0 - jnp.power(2.0, -5.0 - jnp.arange(HEADS, dtype=F32)))
    idx = jnp.arange(chunk, dtype=F32)
    rel = idx[:, None] - idx[None, :]
    causal = rel >= 0
    dmask = jnp.where(causal[None],
                      jnp.exp(log_gamma[:, None, None] * jnp.where(causal, rel, 0.0)[None]), 0.0) * scale
    qdec = jnp.exp(log_gamma[:, None] * (idx + 1.0)[None, :])
    kdec = jnp.exp(log_gamma[:, None] * (chunk - 1 - idx)[None, :]) * scale
    qdec = jnp.broadcast_to(qdec[:, :, None], (HEADS, chunk, QK_DIM))
    kdec = jnp.broadcast_to(kdec[:, None, :], (HEADS, QK_DIM, chunk))
    log_gamma64 = np.log(1.0 - np.power(2.0, -5.0 - np.arange(HEADS, dtype=np.float64)))
    chunk_decay = tuple(float(v) for v in np.exp(log_gamma64 * chunk))
    return dmask, qdec, kdec, chunk_decay


def kernel(x, w_in, b_if, b_merge, conv_w, conv_b, ret_norm_g, mlstm_norm_g, w_proj_ret,
           w_proj_mlstm, w_out, ln1_g, ln1_b, w_gate_up, w_down, ln2_g, ln2_b):
    batch, seq, d_model = x.shape
    depth = w_in.shape[0]
    d_ff = w_down.shape[1]
    alpha = float((2 * depth) ** 0.25)
    assert seq % MIXER_TILE == 0 and d_model % 128 == 0

    cos, sin = _position_tables(seq)
    dmask, qdec, kdec, chunk_decay = _decay_tables(MIXER_TILE)

    def rows(v):
        return v.reshape(depth, 1, -1).astype(F32)

    def weight(w):
        return jnp.pad(w.astype(BF16), ((0, 0), (0, 0), (0, LANE_PAD)))

    off_i = OFF_IF
    w_main = weight(w_in[:, :, :off_i])
    w_gab = weight(w_in[:, :, off_i + 2 * HEADS:])
    w_if = w_in[:, :, off_i:off_i + 2 * HEADS]
    zcol = jnp.zeros((depth, d_model, 8 - HEADS), w_if.dtype)
    ztail = jnp.zeros((depth, d_model, GATE_COLS - GATE_ROWS), w_if.dtype)
    w_if = jnp.concatenate([w_if[:, :, :HEADS], zcol, w_if[:, :, HEADS:], zcol, ztail], axis=-1).astype(BF16)
    zb = jnp.zeros((depth, 8 - HEADS), b_if.dtype)
    zbt = jnp.zeros((depth, GATE_COLS - GATE_ROWS), b_if.dtype)
    bif = jnp.concatenate([b_if[:, :HEADS], zb, b_if[:, HEADS:], zb, zbt], axis=1).astype(F32)
    bif = bif[:, None, :]

    mixer_params = (w_main, w_gab, w_if, bif, rows(b_merge), conv_w.astype(F32), rows(conv_b),
                    rows(ret_norm_g), rows(mlstm_norm_g), weight(w_proj_ret),
                    weight(w_proj_mlstm), weight(w_out), rows(ln1_g), rows(ln1_b))
    ffn_params = (w_gate_up.astype(BF16), w_down.astype(BF16), rows(ln2_g), rows(ln2_b))
    assert w_gate_up.shape[2] == 2 * d_ff and d_ff % 128 == 0

    for l in range(depth):
        x = _mixer_call(x, cos, sin, mixer_params, (dmask, qdec, kdec), layer=l, alpha=alpha,
                        chunk_decay=chunk_decay)
        x2 = _ffn_call(x.reshape(batch * seq, d_model), ffn_params, layer=l, alpha=alpha)
        x = x2.reshape(batch, seq, d_model)
    return x
```

```python
import functools

import jax
import jax.numpy as jnp
import numpy as np
from jax import lax
from jax.experimental import pallas as pl
from jax.experimental.pallas import tpu as pltpu

F32 = jnp.float32
BF16 = jnp.bfloat16

HEADS = 4
QK_DIM = 128
V_DIM = 256
CONV_WIDTH = 4
ROPE_BASE = 10000.0
NORM_EPS = 1e-5
CONV_PAD = 8
GATE_ROWS = 16
GATE_COLS = 128
LANE_PAD = 128

MIXER_TILE = 256
FFN_TILE = 1024
FFN_SUB = 256
VMEM_LIMIT = 56 * 1024 * 1024

_QK = HEADS * QK_DIM
_V = HEADS * V_DIM
OFF_RQ = 0
OFF_RK = OFF_RQ + _QK
OFF_RV = OFF_RK + _QK
OFF_RG = OFF_RV + _V
OFF_MQK = OFF_RG + _V
OFF_MV = OFF_MQK + 2 * _QK
OFF_MO = OFF_MV + _V
OFF_IF = OFF_MO + _V
P_Q = 0
P_K = P_Q + _QK
P_G = P_K + _QK
P_MQK = P_G + _V
P_MO = P_MQK + 2 * _QK
P_WIDTH = P_MO + _V


def _dot(a, b):
    return jnp.dot(a, b, preferred_element_type=F32)


def _sigmoid(x):
    return 1.0 / (1.0 + jnp.exp(-x))


def _layer_norm(y, g, b):
    mu = jnp.mean(y, axis=-1, keepdims=True)
    d = y - mu
    var = jnp.mean(d * d, axis=-1, keepdims=True)
    return d * lax.rsqrt(var + NORM_EPS) * g + b


def _head_norm(o):
    mu = jnp.mean(o, axis=-1, keepdims=True)
    d = o - mu
    var = jnp.mean(d * d, axis=-1, keepdims=True)
    return d * lax.rsqrt(var + NORM_EPS)


def _lane_scan(x, lane, op, fill, width):
    d = 1
    while d < width:
        x = op(x, jnp.where(lane >= d, pltpu.roll(x, d, 1), fill))
        d *= 2
    return x


def _project_pieces(x_ref, wmain_ref, wif_ref, bif_ref, p32_ref, p16_ref, pif_ref):
    xb = x_ref[0].astype(BF16)

    def f32_piece(dst, src, width):
        def run():
            p32_ref[:, dst:dst + width] = _dot(xb, wmain_ref[:, src:src + width])
        return run

    def bf16_piece(dst, src, width):
        def run():
            p16_ref[:, dst:dst + width] = _dot(xb, wmain_ref[:, src:src + width]).astype(BF16)
        return run

    def gate_rows():
        pre = _dot(xb, wif_ref[...]) + bif_ref[...]
        pif_ref[...] = pre.T[0:GATE_ROWS, :]

    return [f32_piece(P_Q, OFF_RQ, 2 * _QK),
            bf16_piece(0, OFF_RV, _V),
            f32_piece(P_G, OFF_RG, _V),
            f32_piece(P_MQK, OFF_MQK, 2 * _QK),
            bf16_piece(_V, OFF_MV, _V),
            f32_piece(P_MO, OFF_MO, _V),
            gate_rows]


def _stage_mix(x_ref, cos, sin, p32_ref, p16_ref, pif_ref, wgab_ref, bmerge_ref, convw_ref,
               convb_ref, rng_ref, mng_ref, wpr_ref, wpm_ref, wo_ref, lng_ref, lnb_ref,
               dmask_ref, qdec_ref, kdec_ref, r_ref, c_ref, n_ref, m_ref, conv_ref, oret_ref,
               oml_ref, *, tm, d_model, alpha, chunk_decay, side_work=()):
    side_work = list(side_work)

    def side():
        if side_work:
            side_work.pop(0)()

    lane = lax.broadcasted_iota(jnp.int32, (8, tm), 1)
    row_i = lax.broadcasted_iota(jnp.int32, (tm, tm), 0)
    col_i = lax.broadcasted_iota(jnp.int32, (tm, tm), 1)
    causal = col_i <= row_i
    neg_inf = jnp.float32(-jnp.inf)

    m_prev = m_ref[:, 0:1]
    i_pre = pif_ref[0:8, :]
    f_pre = pif_ref[8:16, :]
    log_f = jnp.minimum(f_pre, 0.0) - jnp.log1p(jnp.exp(-jnp.abs(f_pre)))
    b = _lane_scan(log_f, lane, jnp.add, 0.0, tm)
    a = i_pre - b
    cm = _lane_scan(a, lane, jnp.maximum, neg_inf, tm)
    g = jnp.maximum(cm, m_prev)
    inter = jnp.exp(m_prev - g)
    exp_nm = jnp.exp(-(b + g))
    b_end = b[:, tm - 1:tm]
    cm_end = cm[:, tm - 1:tm]
    m_loc = b_end + cm_end
    w_end = jnp.exp(a - cm_end)
    m_new = jnp.maximum(b_end + m_prev, m_loc)
    a_c = jnp.broadcast_to(jnp.exp(b_end + m_prev - m_new), (8, V_DIM))
    c_c = jnp.broadcast_to(jnp.exp(m_loc - m_new), (8, V_DIM))
    m_ref[...] = jnp.broadcast_to(m_new, m_ref.shape)
    pad_rows = jnp.zeros((128 - 32, tm), F32)
    zt = jnp.concatenate([g, inter, exp_nm, w_end, pad_rows], axis=0).T

    for h in range(HEADS):
        side()
        vs = slice(h * V_DIM, (h + 1) * V_DIM)
        qc = p32_ref[:, P_Q + h * QK_DIM:P_Q + (h + 1) * QK_DIM]
        kc = p32_ref[:, P_K + h * QK_DIM:P_K + (h + 1) * QK_DIM]
        qc = qc * cos + pltpu.roll(qc, QK_DIM // 2, 1) * sin
        kc = kc * cos + pltpu.roll(kc, QK_DIM // 2, 1) * sin
        vc = p16_ref[:, vs]
        kt = kc.T
        s = _dot(qc.astype(BF16), kt.astype(BF16)) * dmask_ref[h]
        r_state = r_ref[h]
        o = (_dot(s.astype(BF16), vc)
             + _dot((qc * qdec_ref[h]).astype(BF16), r_state.astype(BF16)))
        r_ref[h] = r_state * chunk_decay[h] + _dot((kt * kdec_ref[h]).astype(BF16), vc)
        gate = p32_ref[:, P_G + h * V_DIM:P_G + (h + 1) * V_DIM]
        out = _head_norm(o) * rng_ref[:, vs] * (gate * _sigmoid(gate))
        oret_ref[:, vs] = out.astype(BF16)
    y_ret = _dot(oret_ref[...], wpr_ref[:, 0:d_model])

    mqk = p32_ref[:, P_MQK:P_MQK + 2 * _QK]
    conv_ref[CONV_PAD:CONV_PAD + tm, :] = mqk
    y = convb_ref[...] + convw_ref[CONV_WIDTH - 1:CONV_WIDTH, :] * mqk
    for j in range(CONV_WIDTH - 1):
        lo = CONV_PAD - (CONV_WIDTH - 1) + j
        y = y + convw_ref[j:j + 1, :] * conv_ref[lo:lo + tm, :]
    conv_ref[0:CONV_PAD, :] = mqk[tm - CONV_PAD:tm, :]
    qk_c = y * _sigmoid(y)

    x = x_ref[0]
    xb = x.astype(BF16)

    def merge_gate(k):
        pre = _dot(xb, wgab_ref[:, k * d_model:(k + 1) * d_model])
        return _sigmoid(pre + bmerge_ref[:, k * d_model:(k + 1) * d_model])

    merge_gates = []
    n_prev = n_ref[...]
    for h in range(HEADS):
        side()
        if h >= 2:
            merge_gates.append(merge_gate(h - 2))
        vs = slice(h * V_DIM, (h + 1) * V_DIM)
        qc = qk_c[:, h * QK_DIM:(h + 1) * QK_DIM]
        kc = qk_c[:, _QK + h * QK_DIM:_QK + (h + 1) * QK_DIM] * (QK_DIM ** -0.5)
        vc = p16_ref[:, _V + h * V_DIM:_V + (h + 1) * V_DIM]
        g_col = zt[:, h:h + 1]
        inter_col = zt[:, 8 + h:9 + h]
        expnm_col = zt[:, 16 + h:17 + h]
        wend_col = zt[:, 24 + h:25 + h]
        dmat = jnp.exp(jnp.where(causal, a[h:h + 1, :] - g_col, neg_inf))
        qcb = qc.astype(BF16)
        kt = kc.T
        qk = _dot(qcb, kt.astype(BF16)) * dmat
        c_state = c_ref[h]
        num = _dot(qk.astype(BF16), vc) + _dot(qcb, c_state.astype(BF16)) * inter_col
        den = (jnp.sum(qk, axis=-1, keepdims=True)
               + jnp.sum(qc * n_prev[h:h + 1, :], axis=-1, keepdims=True) * inter_col)
        denom = jnp.maximum(jnp.abs(den), expnm_col)
        hid = num * (1.0 / denom)
        kw = kc * wend_col
        kv = _dot((kt * w_end[h:h + 1, :]).astype(BF16), vc)
        ac_h = a_c[h:h + 1, :]
        cc_h = c_c[h:h + 1, :]
        c_ref[h] = ac_h * c_state + cc_h * kv
        n_ref[h:h + 1, :] = (ac_h[:, 0:QK_DIM] * n_prev[h:h + 1, :]
                             + cc_h[:, 0:QK_DIM] * jnp.sum(kw, axis=0, keepdims=True))
        gate = p32_ref[:, P_MO + h * V_DIM:P_MO + (h + 1) * V_DIM]
        out = _head_norm(hid) * mng_ref[:, vs] * _sigmoid(gate)
        oml_ref[:, vs] = out.astype(BF16)
    y_ml = _dot(oml_ref[...], wpm_ref[:, 0:d_model])

    while side_work:
        side()
    gate_a, gate_b = merge_gates
    mix = _dot((gate_a * y_ret + gate_b * y_ml).astype(BF16), wo_ref[:, 0:d_model])
    return _layer_norm(alpha * x + mix, lng_ref[...], lnb_ref[...])


def _mixer_kernel(x_ref, xn_ref, cos_ref, sin_ref, wmain_ref, wgab_ref, wif_ref, bif_ref,
                  bmerge_ref, convw_ref, convb_ref, rng_ref, mng_ref, wpr_ref, wpm_ref, wo_ref,
                  lng_ref, lnb_ref, dmask_ref, qdec_ref, kdec_ref,
                  o_ref,
                  r_ref, c_ref, n_ref, m_ref, conv_ref, oret_ref, oml_ref,
                  pa32_ref, pa16_ref, paif_ref, pb32_ref, pb16_ref, pbif_ref,
                  *, tm, d_model, alpha, chunk_decay):
    step_id = pl.program_id(1)
    pieces = functools.partial(_project_pieces, wmain_ref=wmain_ref, wif_ref=wif_ref,
                               bif_ref=bif_ref)
    buf_a = dict(p32_ref=pa32_ref, p16_ref=pa16_ref, pif_ref=paif_ref)
    buf_b = dict(p32_ref=pb32_ref, p16_ref=pb16_ref, pif_ref=pbif_ref)
    mix = functools.partial(
        _stage_mix, wgab_ref=wgab_ref, bmerge_ref=bmerge_ref, convw_ref=convw_ref,
        convb_ref=convb_ref, rng_ref=rng_ref, mng_ref=mng_ref, wpr_ref=wpr_ref, wpm_ref=wpm_ref,
        wo_ref=wo_ref, lng_ref=lng_ref, lnb_ref=lnb_ref, dmask_ref=dmask_ref, qdec_ref=qdec_ref,
        kdec_ref=kdec_ref, r_ref=r_ref, c_ref=c_ref, n_ref=n_ref, m_ref=m_ref, conv_ref=conv_ref,
        oret_ref=oret_ref, oml_ref=oml_ref, tm=tm, d_model=d_model, alpha=alpha,
        chunk_decay=chunk_decay)

    @pl.when(step_id == 0)
    def _():
        r_ref[...] = jnp.zeros_like(r_ref)
        c_ref[...] = jnp.zeros_like(c_ref)
        n_ref[...] = jnp.zeros_like(n_ref)
        m_ref[...] = jnp.zeros_like(m_ref)
        conv_ref[0:CONV_PAD, :] = jnp.zeros((CONV_PAD, 2 * _QK), F32)
        for piece in pieces(x_ref, **buf_a):
            piece()

    @pl.when(step_id % 2 == 0)
    def _():
        o_ref[0] = mix(x_ref, cos_ref[...], sin_ref[...], **buf_a, side_work=pieces(xn_ref, **buf_b))

    @pl.when(step_id % 2 == 1)
    def _():
        o_ref[0] = mix(x_ref, cos_ref[...], sin_ref[...], **buf_b, side_work=pieces(xn_ref, **buf_a))


def _ffn_kernel(x_ref, wgu_ref, wd_ref, lng_ref, lnb_ref, o_ref, *, alpha, d_ff):
    for i in range(x_ref.shape[0] // FFN_SUB):
        rows = slice(i * FFN_SUB, (i + 1) * FFN_SUB)
        x = x_ref[rows, :]
        xb = x.astype(BF16)
        gate = _dot(xb, wgu_ref[:, 0:d_ff])
        up = _dot(xb, wgu_ref[:, d_ff:2 * d_ff])
        hidden = (gate * _sigmoid(gate) * up).astype(BF16)
        y = alpha * x + _dot(hidden, wd_ref[...])
        o_ref[rows, :] = _layer_norm(y, lng_ref[...], lnb_ref[...])


def _layer_spec(arr, layer):
    tail = arr.shape[1:]
    zeros = (0,) * len(tail)
    return pl.BlockSpec((None,) + tail, lambda *_: (layer,) + zeros, pipeline_mode=pl.Buffered(1))


def _const_spec(arr):
    zeros = (0,) * arr.ndim
    return pl.BlockSpec(arr.shape, lambda *_: zeros, pipeline_mode=pl.Buffered(1))


def _mixer_call(x, cos, sin, layer_params, tables, *, layer, alpha, chunk_decay):
    batch, seq, d_model = x.shape
    tm = MIXER_TILE
    assert seq % tm == 0
    n_tiles = seq // tm
    kern = functools.partial(_mixer_kernel, tm=tm, d_model=d_model, alpha=alpha,
                             chunk_decay=chunk_decay)
    handoff = [pltpu.VMEM((tm, P_WIDTH), F32),
               pltpu.VMEM((tm, 2 * _V), BF16),
               pltpu.VMEM((GATE_ROWS, tm), F32)]
    return pl.pallas_call(
        kern,
        grid=(batch, n_tiles),
        in_specs=[pl.BlockSpec((1, tm, d_model), lambda b, j: (b, j, 0)),
                  pl.BlockSpec((1, tm, d_model),
                               lambda b, j: (b, jnp.minimum(j + 1, n_tiles - 1), 0)),
                  pl.BlockSpec((tm, QK_DIM), lambda b, j: (j, 0)),
                  pl.BlockSpec((tm, QK_DIM), lambda b, j: (j, 0))]
                 + [_layer_spec(p, layer) for p in layer_params]
                 + [_const_spec(t) for t in tables],
        out_specs=pl.BlockSpec((1, tm, d_model), lambda b, j: (b, j, 0)),
        out_shape=jax.ShapeDtypeStruct(x.shape, x.dtype),
        scratch_shapes=[
            pltpu.VMEM((HEADS, QK_DIM, V_DIM), F32),
            pltpu.VMEM((HEADS, QK_DIM, V_DIM), F32),
            pltpu.VMEM((8, QK_DIM), F32),
            pltpu.VMEM((8, 128), F32),
            pltpu.VMEM((CONV_PAD + tm, 2 * _QK), F32),
            pltpu.VMEM((tm, _V), BF16),
            pltpu.VMEM((tm, _V), BF16),
        ] + handoff + handoff,
        compiler_params=pltpu.CompilerParams(
            dimension_semantics=("arbitrary", "arbitrary"), vmem_limit_bytes=VMEM_LIMIT),
        name="mixer",
    )(x, x, cos, sin, *layer_params, *tables)


def _ffn_call(x2d, layer_params, *, layer, alpha):
    tokens, d_model = x2d.shape
    tm = min(FFN_TILE, tokens)
    assert tokens % tm == 0 and tm % FFN_SUB == 0
    d_ff = layer_params[1].shape[1]
    return pl.pallas_call(
        functools.partial(_ffn_kernel, alpha=alpha, d_ff=d_ff),
        grid=(tokens // tm,),
        in_specs=[pl.BlockSpec((tm, d_model), lambda i: (i, 0))]
                 + [_layer_spec(p, layer) for p in layer_params],
        out_specs=pl.BlockSpec((tm, d_model), lambda i: (i, 0)),
        out_shape=jax.ShapeDtypeStruct(x2d.shape, x2d.dtype),
        compiler_params=pltpu.CompilerParams(
            dimension_semantics=("arbitrary",), vmem_limit_bytes=VMEM_LIMIT),
        name="ffn",
    )(x2d, *layer_params)


def _position_tables(seq):
    half = QK_DIM // 2
    inv_freq = ROPE_BASE ** (-jnp.arange(half, dtype=F32) / half)
    ang = jnp.arange(seq, dtype=jnp.int32).astype(F32)[:, None] * inv_freq[None, :]
    cos = jnp.cos(ang)
    sin = jnp.sin(ang)
    return jnp.concatenate([cos, cos], axis=-1), jnp.concatenate([-sin, sin], axis=-1)


def _decay_tables(chunk):
    scale = QK_DIM ** -0.5
    log_gamma = jnp.log(1.0 - jnp.power(2.0, -5.0 - jnp.arange(HEADS, dtype=F32)))
    idx = jnp.arange(chunk, dtype=F32)
    rel = idx[:, None] - idx[None, :]
    causal = rel >= 0
    dmask = jnp.where(causal[None],
                      jnp.exp(log_gamma[:, None, None] * jnp.where(causal, rel, 0.0)[None]), 0.0) * scale
    qdec = jnp.exp(log_gamma[:, None] * (idx + 1.0)[None, :])
    kdec = jnp.exp(log_gamma[:, None] * (chunk - 1 - idx)[None, :]) * scale
    qdec = jnp.broadcast_to(qdec[:, :, None], (HEADS, chunk, QK_DIM))
    kdec = jnp.broadcast_to(kdec[:, None, :], (HEADS, QK_DIM, chunk))
    log_gamma64 = np.log(1.0 - np.power(2.0, -5.0 - np.arange(HEADS, dtype=np.float64)))
    chunk_decay = tuple(float(v) for v in np.exp(log_gamma64 * chunk))
    return dmask, qdec, kdec, chunk_decay


def kernel(x, w_in, b_if, b_merge, conv_w, conv_b, ret_norm_g, mlstm_norm_g, w_proj_ret,
           w_proj_mlstm, w_out, ln1_g, ln1_b, w_gate_up, w_down, ln2_g, ln2_b):
    batch, seq, d_model = x.shape
    depth = w_in.shape[0]
    d_ff = w_down.shape[1]
    alpha = float((2 * depth) ** 0.25)
    assert seq % MIXER_TILE == 0 and d_model % 128 == 0

    with jax.ensure_compile_time_eval():
        cos, sin = _position_tables(seq)
        dmask, qdec, kdec, chunk_decay = _decay_tables(MIXER_TILE)

    def rows(v):
        return v.reshape(depth, 1, -1).astype(F32)

    def weight(w):
        return jnp.pad(w.astype(BF16), ((0, 0), (0, 0), (0, LANE_PAD)))

    off_i = OFF_IF
    w_main = weight(w_in[:, :, :off_i])
    w_gab = weight(w_in[:, :, off_i + 2 * HEADS:])
    w_if = w_in[:, :, off_i:off_i + 2 * HEADS]
    zcol = jnp.zeros((depth, d_model, 8 - HEADS), w_if.dtype)
    ztail = jnp.zeros((depth, d_model, GATE_COLS - GATE_ROWS), w_if.dtype)
    w_if = jnp.concatenate([w_if[:, :, :HEADS], zcol, w_if[:, :, HEADS:], zcol, ztail], axis=-1).astype(BF16)
    zb = jnp.zeros((depth, 8 - HEADS), b_if.dtype)
    zbt = jnp.zeros((depth, GATE_COLS - GATE_ROWS), b_if.dtype)
    bif = jnp.concatenate([b_if[:, :HEADS], zb, b_if[:, HEADS:], zb, zbt], axis=1).astype(F32)
    bif = bif[:, None, :]

    mixer_params = (w_main, w_gab, w_if, bif, rows(b_merge), conv_w.astype(F32), rows(conv_b),
                    rows(ret_norm_g), rows(mlstm_norm_g), weight(w_proj_ret),
                    weight(w_proj_mlstm), weight(w_out), rows(ln1_g), rows(ln1_b))
    ffn_params = (w_gate_up.astype(BF16), w_down.astype(BF16), rows(ln2_g), rows(ln2_b))
    assert w_gate_up.shape[2] == 2 * d_ff and d_ff % 128 == 0

    for l in range(depth):
        x = _mixer_call(x, cos, sin, mixer_params, (dmask, qdec, kdec), layer=l, alpha=alpha,
                        chunk_decay=chunk_decay)
        x2 = _ffn_call(x.reshape(batch * seq, d_model), ffn_params, layer=l, alpha=alpha)
        x = x2.reshape(batch, seq, d_model)
    return x
```

```python
import functools

import jax
import jax.numpy as jnp
import numpy as np
from jax import lax
from jax.experimental import pallas as pl
from jax.experimental.pallas import tpu as pltpu

F32 = jnp.float32
BF16 = jnp.bfloat16

HEADS = 4
QK_DIM = 128
V_DIM = 256
CONV_WIDTH = 4
ROPE_BASE = 10000.0
NORM_EPS = 1e-5
CONV_PAD = 8
GATE_ROWS = 16
GATE_COLS = 128
LANE_PAD = 128

MIXER_TILE = 256
FFN_TILE = 1024
FFN_SUB = 256
VMEM_LIMIT = 56 * 1024 * 1024

_QK = HEADS * QK_DIM
_V = HEADS * V_DIM
OFF_RQ = 0
OFF_RK = OFF_RQ + _QK
OFF_RV = OFF_RK + _QK
OFF_RG = OFF_RV + _V
OFF_MQK = OFF_RG + _V
OFF_MV = OFF_MQK + 2 * _QK
OFF_MO = OFF_MV + _V
OFF_IF = OFF_MO + _V
P_Q = 0
P_K = P_Q + _QK
P_G = P_K + _QK
P_MQK = P_G + _V
P_MO = P_MQK + 2 * _QK
P_WIDTH = P_MO + _V


def _dot(a, b):
    return jnp.dot(a, b, preferred_element_type=F32)


def _sigmoid(x):
    return 0.5 * jnp.tanh(0.5 * x) + 0.5


def _layer_norm(y, g, b):
    mu = jnp.mean(y, axis=-1, keepdims=True)
    d = y - mu
    var = jnp.mean(d * d, axis=-1, keepdims=True)
    return d * lax.rsqrt(var + NORM_EPS) * g + b


def _head_norm(o):
    mu = jnp.mean(o, axis=-1, keepdims=True)
    d = o - mu
    var = jnp.mean(d * d, axis=-1, keepdims=True)
    return d * lax.rsqrt(var + NORM_EPS)


def _lane_scan(x, lane, op, fill, width):
    d = 1
    while d < width:
        x = op(x, jnp.where(lane >= d, pltpu.roll(x, d, 1), fill))
        d *= 2
    return x


def _project_pieces(x_ref, wmain_ref, wif_ref, bif_ref, p32_ref, p16_ref, pif_ref):
    xb = x_ref[0].astype(BF16)

    def f32_piece(dst, src, width):
        def run():
            p32_ref[:, dst:dst + width] = _dot(xb, wmain_ref[:, src:src + width])
        return run

    def bf16_piece(dst, src, width):
        def run():
            p16_ref[:, dst:dst + width] = _dot(xb, wmain_ref[:, src:src + width]).astype(BF16)
        return run

    def gate_rows():
        pre = _dot(xb, wif_ref[...]) + bif_ref[...]
        pif_ref[...] = pre.T[0:GATE_ROWS, :]

    return [f32_piece(P_Q, OFF_RQ, 2 * _QK),
            bf16_piece(0, OFF_RV, _V),
            f32_piece(P_G, OFF_RG, _V),
            f32_piece(P_MQK, OFF_MQK, 2 * _QK),
            bf16_piece(_V, OFF_MV, _V),
            f32_piece(P_MO, OFF_MO, _V),
            gate_rows]


def _stage_mix(x_ref, cos, sin, p32_ref, p16_ref, pif_ref, wgab_ref, bmerge_ref, convw_ref,
               convb_ref, rng_ref, mng_ref, wpr_ref, wpm_ref, wo_ref, lng_ref, lnb_ref,
               dmask_ref, qdec_ref, kdec_ref, r_ref, c_ref, n_ref, m_ref, conv_ref, oret_ref,
               oml_ref, *, tm, d_model, alpha, chunk_decay, side_work=()):
    side_work = list(side_work)

    def side():
        if side_work:
            side_work.pop(0)()

    lane = lax.broadcasted_iota(jnp.int32, (8, tm), 1)
    row_i = lax.broadcasted_iota(jnp.int32, (tm, tm), 0)
    col_i = lax.broadcasted_iota(jnp.int32, (tm, tm), 1)
    causal = col_i <= row_i
    neg_inf = jnp.float32(-jnp.inf)

    m_prev = m_ref[:, 0:1]
    i_pre = pif_ref[0:8, :]
    f_pre = pif_ref[8:16, :]
    log_f = jnp.minimum(f_pre, 0.0) - jnp.log1p(jnp.exp(-jnp.abs(f_pre)))
    b = _lane_scan(log_f, lane, jnp.add, 0.0, tm)
    a = i_pre - b
    cm = _lane_scan(a, lane, jnp.maximum, neg_inf, tm)
    g = jnp.maximum(cm, m_prev)
    inter = jnp.exp(m_prev - g)
    exp_nm = jnp.exp(-(b + g))
    b_end = b[:, tm - 1:tm]
    cm_end = cm[:, tm - 1:tm]
    m_loc = b_end + cm_end
    w_end = jnp.exp(a - cm_end)
    m_new = jnp.maximum(b_end + m_prev, m_loc)
    a_c = jnp.broadcast_to(jnp.exp(b_end + m_prev - m_new), (8, V_DIM))
    c_c = jnp.broadcast_to(jnp.exp(m_loc - m_new), (8, V_DIM))
    m_ref[...] = jnp.broadcast_to(m_new, m_ref.shape)
    pad_rows = jnp.zeros((128 - 32, tm), F32)
    zt = jnp.concatenate([g, inter, exp_nm, w_end, pad_rows], axis=0).T

    for h in range(HEADS):
        side()
        vs = slice(h * V_DIM, (h + 1) * V_DIM)
        qc = p32_ref[:, P_Q + h * QK_DIM:P_Q + (h + 1) * QK_DIM]
        kc = p32_ref[:, P_K + h * QK_DIM:P_K + (h + 1) * QK_DIM]
        qc = qc * cos + pltpu.roll(qc, QK_DIM // 2, 1) * sin
        kc = kc * cos + pltpu.roll(kc, QK_DIM // 2, 1) * sin
        vc = p16_ref[:, vs]
        kt = kc.T
        s = _dot(qc.astype(BF16), kt.astype(BF16)) * dmask_ref[h]
        r_state = r_ref[h]
        o = (_dot(s.astype(BF16), vc)
             + _dot((qc * qdec_ref[h]).astype(BF16), r_state.astype(BF16)))
        r_ref[h] = r_state * chunk_decay[h] + _dot((kt * kdec_ref[h]).astype(BF16), vc)
        gate = p32_ref[:, P_G + h * V_DIM:P_G + (h + 1) * V_DIM]
        out = _head_norm(o) * rng_ref[:, vs] * (gate * _sigmoid(gate))
        oret_ref[:, vs] = out.astype(BF16)
    y_ret = _dot(oret_ref[...], wpr_ref[:, 0:d_model])

    mqk = p32_ref[:, P_MQK:P_MQK + 2 * _QK]
    conv_ref[CONV_PAD:CONV_PAD + tm, :] = mqk
    y = convb_ref[...] + convw_ref[CONV_WIDTH - 1:CONV_WIDTH, :] * mqk
    for j in range(CONV_WIDTH - 1):
        lo = CONV_PAD - (CONV_WIDTH - 1) + j
        y = y + convw_ref[j:j + 1, :] * conv_ref[lo:lo + tm, :]
    conv_ref[0:CONV_PAD, :] = mqk[tm - CONV_PAD:tm, :]
    qk_c = y * _sigmoid(y)

    x = x_ref[0]
    xb = x.astype(BF16)

    def merge_gate(k):
        pre = _dot(xb, wgab_ref[:, k * d_model:(k + 1) * d_model])
        return _sigmoid(pre + bmerge_ref[:, k * d_model:(k + 1) * d_model])

    merge_gates = []
    n_prev = n_ref[...]
    for h in range(HEADS):
        side()
        if h >= 2:
            merge_gates.append(merge_gate(h - 2))
        vs = slice(h * V_DIM, (h + 1) * V_DIM)
        qc = qk_c[:, h * QK_DIM:(h + 1) * QK_DIM]
        kc = qk_c[:, _QK + h * QK_DIM:_QK + (h + 1) * QK_DIM] * (QK_DIM ** -0.5)
        vc = p16_ref[:, _V + h * V_DIM:_V + (h + 1) * V_DIM]
        g_col = zt[:, h:h + 1]
        inter_col = zt[:, 8 + h:9 + h]
        expnm_col = zt[:, 16 + h:17 + h]
        wend_col = zt[:, 24 + h:25 + h]
        dmat = jnp.exp(jnp.where(causal, a[h:h + 1, :] - g_col, neg_inf))
        qcb = qc.astype(BF16)
        kt = kc.T
        qk = _dot(qcb, kt.astype(BF16)) * dmat
        c_state = c_ref[h]
        num = _dot(qk.astype(BF16), vc) + _dot(qcb, c_state.astype(BF16)) * inter_col
        den = (jnp.sum(qk, axis=-1, keepdims=True)
               + jnp.sum(qc * n_prev[h:h + 1, :], axis=-1, keepdims=True) * inter_col)
        denom = jnp.maximum(jnp.abs(den), expnm_col)
        hid = num * (1.0 / denom)
        kw = kc * wend_col
        kv = _dot((kt * w_end[h:h + 1, :]).astype(BF16), vc)
        ac_h = a_c[h:h + 1, :]
        cc_h = c_c[h:h + 1, :]
        c_ref[h] = ac_h * c_state + cc_h * kv
        n_ref[h:h + 1, :] = (ac_h[:, 0:QK_DIM] * n_prev[h:h + 1, :]
                             + cc_h[:, 0:QK_DIM] * jnp.sum(kw, axis=0, keepdims=True))
        gate = p32_ref[:, P_MO + h * V_DIM:P_MO + (h + 1) * V_DIM]
        out = _head_norm(hid) * mng_ref[:, vs] * _sigmoid(gate)
        oml_ref[:, vs] = out.astype(BF16)
    y_ml = _dot(oml_ref[...], wpm_ref[:, 0:d_model])

    while side_work:
        side()
    gate_a, gate_b = merge_gates
    mix = _dot((gate_a * y_ret + gate_b * y_ml).astype(BF16), wo_ref[:, 0:d_model])
    return _layer_norm(alpha * x + mix, lng_ref[...], lnb_ref[...])


def _mixer_kernel(x_ref, xn_ref, cos_ref, sin_ref, wmain_ref, wgab_ref, wif_ref, bif_ref,
                  bmerge_ref, convw_ref, convb_ref, rng_ref, mng_ref, wpr_ref, wpm_ref, wo_ref,
                  lng_ref, lnb_ref, dmask_ref, qdec_ref, kdec_ref,
                  o_ref,
                  r_ref, c_ref, n_ref, m_ref, conv_ref, oret_ref, oml_ref,
                  pa32_ref, pa16_ref, paif_ref, pb32_ref, pb16_ref, pbif_ref,
                  *, tm, d_model, alpha, chunk_decay):
    step_id = pl.program_id(1)
    pieces = functools.partial(_project_pieces, wmain_ref=wmain_ref, wif_ref=wif_ref,
                               bif_ref=bif_ref)
    buf_a = dict(p32_ref=pa32_ref, p16_ref=pa16_ref, pif_ref=paif_ref)
    buf_b = dict(p32_ref=pb32_ref, p16_ref=pb16_ref, pif_ref=pbif_ref)
    mix = functools.partial(
        _stage_mix, wgab_ref=wgab_ref, bmerge_ref=bmerge_ref, convw_ref=convw_ref,
        convb_ref=convb_ref, rng_ref=rng_ref, mng_ref=mng_ref, wpr_ref=wpr_ref, wpm_ref=wpm_ref,
        wo_ref=wo_ref, lng_ref=lng_ref, lnb_ref=lnb_ref, dmask_ref=dmask_ref, qdec_ref=qdec_ref,
        kdec_ref=kdec_ref, r_ref=r_ref, c_ref=c_ref, n_ref=n_ref, m_ref=m_ref, conv_ref=conv_ref,
        oret_ref=oret_ref, oml_ref=oml_ref, tm=tm, d_model=d_model, alpha=alpha,
        chunk_decay=chunk_decay)

    @pl.when(step_id == 0)
    def _():
        r_ref[...] = jnp.zeros_like(r_ref)
        c_ref[...] = jnp.zeros_like(c_ref)
        n_ref[...] = jnp.zeros_like(n_ref)
        m_ref[...] = jnp.zeros_like(m_ref)
        conv_ref[0:CONV_PAD, :] = jnp.zeros((CONV_PAD, 2 * _QK), F32)
        for piece in pieces(x_ref, **buf_a):
            piece()

    @pl.when(step_id % 2 == 0)
    def _():
        o_ref[0] = mix(x_ref, cos_ref[...], sin_ref[...], **buf_a, side_work=pieces(xn_ref, **buf_b))

    @pl.when(step_id % 2 == 1)
    def _():
        o_ref[0] = mix(x_ref, cos_ref[...], sin_ref[...], **buf_b, side_work=pieces(xn_ref, **buf_a))


def _ffn_kernel(x_ref, wgu_ref, wd_ref, lng_ref, lnb_ref, o_ref, *, alpha, d_ff):
    for i in range(x_ref.shape[0] // FFN_SUB):
        rows = slice(i * FFN_SUB, (i + 1) * FFN_SUB)
        x = x_ref[rows, :]
        xb = x.astype(BF16)
        gate = _dot(xb, wgu_ref[:, 0:d_ff])
        up = _dot(xb, wgu_ref[:, d_ff:2 * d_ff])
        hidden = (gate * _sigmoid(gate) * up).astype(BF16)
        y = alpha * x + _dot(hidden, wd_ref[...])
        o_ref[rows, :] = _layer_norm(y, lng_ref[...], lnb_ref[...])


def _layer_spec(arr, layer):
    tail = arr.shape[1:]
    zeros = (0,) * len(tail)
    return pl.BlockSpec((None,) + tail, lambda *_: (layer,) + zeros, pipeline_mode=pl.Buffered(1))


def _const_spec(arr):
    zeros = (0,) * arr.ndim
    return pl.BlockSpec(arr.shape, lambda *_: zeros, pipeline_mode=pl.Buffered(1))


def _mixer_call(x, cos, sin, layer_params, tables, *, layer, alpha, chunk_decay):
    batch, seq, d_model = x.shape
    tm = MIXER_TILE
    assert seq % tm == 0
    n_tiles = seq // tm
    kern = functools.partial(_mixer_kernel, tm=tm, d_model=d_model, alpha=alpha,
                             chunk_decay=chunk_decay)
    handoff = [pltpu.VMEM((tm, P_WIDTH), F32),
               pltpu.VMEM((tm, 2 * _V), BF16),
               pltpu.VMEM((GATE_ROWS, tm), F32)]
    return pl.pallas_call(
        kern,
        grid=(batch, n_tiles),
        in_specs=[pl.BlockSpec((1, tm, d_model), lambda b, j: (b, j, 0)),
                  pl.BlockSpec((1, tm, d_model),
                               lambda b, j: (b, jnp.minimum(j + 1, n_tiles - 1), 0)),
                  pl.BlockSpec((tm, QK_DIM), lambda b, j: (j, 0)),
                  pl.BlockSpec((tm, QK_DIM), lambda b, j: (j, 0))]
                 + [_layer_spec(p, layer) for p in layer_params]
                 + [_const_spec(t) for t in tables],
        out_specs=pl.BlockSpec((1, tm, d_model), lambda b, j: (b, j, 0)),
        out_shape=jax.ShapeDtypeStruct(x.shape, x.dtype),
        scratch_shapes=[
            pltpu.VMEM((HEADS, QK_DIM, V_DIM), F32),
            pltpu.VMEM((HEADS, QK_DIM, V_DIM), F32),
            pltpu.VMEM((8, QK_DIM), F32),
            pltpu.VMEM((8, 128), F32),
            pltpu.VMEM((CONV_PAD + tm, 2 * _QK), F32),
            pltpu.VMEM((tm, _V), BF16),
            pltpu.VMEM((tm, _V), BF16),
        ] + handoff + handoff,
        compiler_params=pltpu.CompilerParams(
            dimension_semantics=("arbitrary", "arbitrary"), vmem_limit_bytes=VMEM_LIMIT),
        name="mixer",
    )(x, x, cos, sin, *layer_params, *tables)


def _ffn_call(x2d, layer_params, *, layer, alpha):
    tokens, d_model = x2d.shape
    tm = min(FFN_TILE, tokens)
    assert tokens % tm == 0 and tm % FFN_SUB == 0
    d_ff = layer_params[1].shape[1]
    return pl.pallas_call(
        functools.partial(_ffn_kernel, alpha=alpha, d_ff=d_ff),
        grid=(tokens // tm,),
        in_specs=[pl.BlockSpec((tm, d_model), lambda i: (i, 0))]
                 + [_layer_spec(p, layer) for p in layer_params],
        out_specs=pl.BlockSpec((tm, d_model), lambda i: (i, 0)),
        out_shape=jax.ShapeDtypeStruct(x2d.shape, x2d.dtype),
        compiler_params=pltpu.CompilerParams(
            dimension_semantics=("arbitrary",), vmem_limit_bytes=VMEM_LIMIT),
        name="ffn",
    )(x2d, *layer_params)


def _position_tables(seq):
    half = QK_DIM // 2
    inv_freq = ROPE_BASE ** (-jnp.arange(half, dtype=F32) / half)
    ang = jnp.arange(seq, dtype=jnp.int32).astype(F32)[:, None] * inv_freq[None, :]
    cos = jnp.cos(ang)
    sin = jnp.sin(ang)
    return jnp.concatenate([cos, cos], axis=-1), jnp.concatenate([-sin, sin], axis=-1)


def _decay_tables(chunk):
    scale = QK_DIM ** -0.5
    log_gamma = jnp.log(1.0 - jnp.power(2.0, -5.0 - jnp.arange(HEADS, dtype=F32)))
    idx = jnp.arange(chunk, dtype=F32)
    rel = idx[:, None] - idx[None, :]
    causal = rel >= 0
    dmask = jnp.where(causal[None],
                      jnp.exp(log_gamma[:, None, None] * jnp.where(causal, rel, 0.0)[None]), 0.0) * scale
    qdec = jnp.exp(log_gamma[:, None] * (idx + 1.0)[None, :])
    kdec = jnp.exp(log_gamma[:, None] * (chunk - 1 - idx)[None, :]) * scale
    qdec = jnp.broadcast_to(qdec[:, :, None], (HEADS, chunk, QK_DIM))
    kdec = jnp.broadcast_to(kdec[:, None, :], (HEADS, QK_DIM, chunk))
    log_gamma64 = np.log(1.0 - np.power(2.0, -5.0 - np.arange(HEADS, dtype=np.float64)))
    chunk_decay = tuple(float(v) for v in np.exp(log_gamma64 * chunk))
    return dmask, qdec, kdec, chunk_decay


def kernel(x, w_in, b_if, b_merge, conv_w, conv_b, ret_norm_g, mlstm_norm_g, w_proj_ret,
           w_proj_mlstm, w_out, ln1_g, ln1_b, w_gate_up, w_down, ln2_g, ln2_b):
    batch, seq, d_model = x.shape
    depth = w_in.shape[0]
    d_ff = w_down.shape[1]
    alpha = float((2 * depth) ** 0.25)
    assert seq % MIXER_TILE == 0 and d_model % 128 == 0

    with jax.ensure_compile_time_eval():
        cos, sin = _position_tables(seq)
        dmask, qdec, kdec, chunk_decay = _decay_tables(MIXER_TILE)

    def rows(v):
        return v.reshape(depth, 1, -1).astype(F32)

    def weight(w):
        return jnp.pad(w.astype(BF16), ((0, 0), (0, 0), (0, LANE_PAD)))

    off_i = OFF_IF
    w_main = weight(w_in[:, :, :off_i])
    w_gab = weight(w_in[:, :, off_i + 2 * HEADS:])
    w_if = w_in[:, :, off_i:off_i + 2 * HEADS]
    zcol = jnp.zeros((depth, d_model, 8 - HEADS), w_if.dtype)
    ztail = jnp.zeros((depth, d_model, GATE_COLS - GATE_ROWS), w_if.dtype)
    w_if = jnp.concatenate([w_if[:, :, :HEADS], zcol, w_if[:, :, HEADS:], zcol, ztail], axis=-1).astype(BF16)
    zb = jnp.zeros((depth, 8 - HEADS), b_if.dtype)
    zbt = jnp.zeros((depth, GATE_COLS - GATE_ROWS), b_if.dtype)
    bif = jnp.concatenate([b_if[:, :HEADS], zb, b_if[:, HEADS:], zb, zbt], axis=1).astype(F32)
    bif = bif[:, None, :]

    mixer_params = (w_main, w_gab, w_if, bif, rows(b_merge), conv_w.astype(F32), rows(conv_b),
                    rows(ret_norm_g), rows(mlstm_norm_g), weight(w_proj_ret),
                    weight(w_proj_mlstm), weight(w_out), rows(ln1_g), rows(ln1_b))
    ffn_params = (w_gate_up.astype(BF16), w_down.astype(BF16), rows(ln2_g), rows(ln2_b))
    assert w_gate_up.shape[2] == 2 * d_ff and d_ff % 128 == 0

    for l in range(depth):
        x = _mixer_call(x, cos, sin, mixer_params, (dmask, qdec, kdec), layer=l, alpha=alpha,
                        chunk_decay=chunk_decay)
        x2 = _ffn_call(x.reshape(batch * seq, d_model), ffn_params, layer=l, alpha=alpha)
        x = x2.reshape(batch, seq, d_model)
    return x
```

```python
import functools

import jax
import jax.numpy as jnp
import numpy as np
from jax import lax
from jax.experimental import pallas as pl
from jax.experimental.pallas import tpu as pltpu

F32 = jnp.float32
BF16 = jnp.bfloat16

HEADS = 4
QK_DIM = 128
V_DIM = 256
CONV_WIDTH = 4
ROPE_BASE = 10000.0
NORM_EPS = 1e-5
CONV_PAD = 8
GATE_ROWS = 16
GATE_COLS = 128
LANE_PAD = 128

MIXER_TILE = 256
FFN_TILE = 1024
FFN_SUB = 256
VMEM_LIMIT = 56 * 1024 * 1024

_QK = HEADS * QK_DIM
_V = HEADS * V_DIM
OFF_RQ = 0
OFF_RK = OFF_RQ + _QK
OFF_RV = OFF_RK + _QK
OFF_RG = OFF_RV + _V
OFF_MQK = OFF_RG + _V
OFF_MV = OFF_MQK + 2 * _QK
OFF_MO = OFF_MV + _V
OFF_IF = OFF_MO + _V
P_Q = 0
P_K = P_Q + _QK
P_G = P_K + _QK
P_MQK = P_G + _V
P_MO = P_MQK + 2 * _QK
P_WIDTH = P_MO + _V


def _dot(a, b):
    return jnp.dot(a, b, preferred_element_type=F32)


def _sigmoid(x):
    return 0.5 * jnp.tanh(0.5 * x) + 0.5


def _layer_norm(y, g, b):
    mu = jnp.mean(y, axis=-1, keepdims=True)
    d = y - mu
    var = jnp.mean(d * d, axis=-1, keepdims=True)
    return d * lax.rsqrt(var + NORM_EPS) * g + b


def _head_norm(o):
    mu = jnp.mean(o, axis=-1, keepdims=True)
    d = o - mu
    var = jnp.mean(d * d, axis=-1, keepdims=True)
    return d * lax.rsqrt(var + NORM_EPS)


def _lane_scan(x, lane, op, fill, width):
    d = 1
    while d < width:
        x = op(x, jnp.where(lane >= d, pltpu.roll(x, d, 1), fill))
        d *= 2
    return x


def _project_pieces(x_ref, wmain_ref, wif_ref, bif_ref, p32_ref, p16_ref, pif_ref):
    xb = x_ref[0].astype(BF16)

    def f32_piece(dst, src, width):
        def run():
            p32_ref[:, dst:dst + width] = _dot(xb, wmain_ref[:, src:src + width])
        return run

    def bf16_piece(dst, src, width):
        def run():
            p16_ref[:, dst:dst + width] = _dot(xb, wmain_ref[:, src:src + width]).astype(BF16)
        return run

    def gate_rows():
        pre = _dot(xb, wif_ref[...]) + bif_ref[...]
        pif_ref[...] = pre.T[0:GATE_ROWS, :]

    return [f32_piece(P_Q, OFF_RQ, 2 * _QK),
            bf16_piece(0, OFF_RV, _V),
            f32_piece(P_G, OFF_RG, _V),
            f32_piece(P_MQK, OFF_MQK, 2 * _QK),
            bf16_piece(_V, OFF_MV, _V),
            f32_piece(P_MO, OFF_MO, _V),
            gate_rows]


def _stage_mix(x_ref, cos, sin, p32_ref, p16_ref, pif_ref, wgab_ref, bmerge_ref, convw_ref,
               convb_ref, rng_ref, mng_ref, wpr_ref, wpm_ref, wo_ref, lng_ref, lnb_ref,
               dmask_ref, qdec_ref, kdec_ref, r_ref, c_ref, n_ref, m_ref, conv_ref, oret_ref,
               oml_ref, *, tm, d_model, alpha, chunk_decay, side_work=()):
    side_work = list(side_work)

    def side():
        if side_work:
            side_work.pop(0)()

    lane = lax.broadcasted_iota(jnp.int32, (8, tm), 1)
    row_i = lax.broadcasted_iota(jnp.int32, (tm, tm), 0)
    col_i = lax.broadcasted_iota(jnp.int32, (tm, tm), 1)
    causal = col_i <= row_i
    neg_inf = jnp.float32(-jnp.inf)

    m_prev = m_ref[:, 0:1]
    i_pre = pif_ref[0:8, :]
    f_pre = pif_ref[8:16, :]
    log_f = jnp.minimum(f_pre, 0.0) - jnp.log1p(jnp.exp(-jnp.abs(f_pre)))
    b = _lane_scan(log_f, lane, jnp.add, 0.0, tm)
    a = i_pre - b
    cm = _lane_scan(a, lane, jnp.maximum, neg_inf, tm)
    g = jnp.maximum(cm, m_prev)
    inter = jnp.exp(m_prev - g)
    exp_nm = jnp.exp(-(b + g))
    b_end = b[:, tm - 1:tm]
    cm_end = cm[:, tm - 1:tm]
    m_loc = b_end + cm_end
    w_end = jnp.exp(a - cm_end)
    m_new = jnp.maximum(b_end + m_prev, m_loc)
    a_c = jnp.broadcast_to(jnp.exp(b_end + m_prev - m_new), (8, V_DIM))
    c_c = jnp.broadcast_to(jnp.exp(m_loc - m_new), (8, V_DIM))
    m_ref[...] = jnp.broadcast_to(m_new, m_ref.shape)
    pad_rows = jnp.zeros((128 - 32, tm), F32)
    zt = jnp.concatenate([g, inter, exp_nm, w_end, pad_rows], axis=0).T

    for h in range(HEADS):
        side()
        vs = slice(h * V_DIM, (h + 1) * V_DIM)
        qc = p32_ref[:, P_Q + h * QK_DIM:P_Q + (h + 1) * QK_DIM]
        kc = p32_ref[:, P_K + h * QK_DIM:P_K + (h + 1) * QK_DIM]
        qc = qc * cos + pltpu.roll(qc, QK_DIM // 2, 1) * sin
        kc = kc * cos + pltpu.roll(kc, QK_DIM // 2, 1) * sin
        vc = p16_ref[:, vs]
        kt = kc.T
        s = _dot(qc.astype(BF16), kt.astype(BF16)) * dmask_ref[h]
        r_state = r_ref[h]
        o = (_dot(s.astype(BF16), vc)
             + _dot((qc * qdec_ref[h]).astype(BF16), r_state.astype(BF16)))
        r_ref[h] = r_state * chunk_decay[h] + _dot((kt * kdec_ref[h]).astype(BF16), vc)
        gate = p32_ref[:, P_G + h * V_DIM:P_G + (h + 1) * V_DIM]
        out = _head_norm(o) * rng_ref[:, vs] * (gate * _sigmoid(gate))
        oret_ref[:, vs] = out.astype(BF16)
    y_ret = _dot(oret_ref[...], wpr_ref[:, 0:d_model])

    mqk = p32_ref[:, P_MQK:P_MQK + 2 * _QK]
    conv_ref[CONV_PAD:CONV_PAD + tm, :] = mqk
    y = convb_ref[...] + convw_ref[CONV_WIDTH - 1:CONV_WIDTH, :] * mqk
    for j in range(CONV_WIDTH - 1):
        lo = CONV_PAD - (CONV_WIDTH - 1) + j
        y = y + convw_ref[j:j + 1, :] * conv_ref[lo:lo + tm, :]
    conv_ref[0:CONV_PAD, :] = mqk[tm - CONV_PAD:tm, :]
    qk_c = y * _sigmoid(y)

    x = x_ref[0]
    xb = x.astype(BF16)

    def merge_gate(k):
        pre = _dot(xb, wgab_ref[:, k * d_model:(k + 1) * d_model])
        return _sigmoid(pre + bmerge_ref[:, k * d_model:(k + 1) * d_model])

    merge_gates = []
    n_prev = n_ref[...]
    for h in range(HEADS):
        side()
        if h >= 2:
            merge_gates.append(merge_gate(h - 2))
        vs = slice(h * V_DIM, (h + 1) * V_DIM)
        qc = qk_c[:, h * QK_DIM:(h + 1) * QK_DIM]
        kc = qk_c[:, _QK + h * QK_DIM:_QK + (h + 1) * QK_DIM] * (QK_DIM ** -0.5)
        vc = p16_ref[:, _V + h * V_DIM:_V + (h + 1) * V_DIM]
        g_col = zt[:, h:h + 1]
        inter_col = zt[:, 8 + h:9 + h]
        expnm_col = zt[:, 16 + h:17 + h]
        wend_col = zt[:, 24 + h:25 + h]
        dmat = jnp.exp(jnp.where(causal, a[h:h + 1, :] - g_col, neg_inf))
        qcb = qc.astype(BF16)
        kt = kc.T
        qk = _dot(qcb, kt.astype(BF16)) * dmat
        c_state = c_ref[h]
        num = _dot(qk.astype(BF16), vc) + _dot(qcb, c_state.astype(BF16)) * inter_col
        den = (jnp.sum(qk, axis=-1, keepdims=True)
               + jnp.sum(qc * n_prev[h:h + 1, :], axis=-1, keepdims=True) * inter_col)
        denom = jnp.maximum(jnp.abs(den), expnm_col)
        hid = num * (1.0 / denom)
        kw = kc * wend_col
        kv = _dot((kt * w_end[h:h + 1, :]).astype(BF16), vc)
        ac_h = a_c[h:h + 1, :]
        cc_h = c_c[h:h + 1, :]
        c_ref[h] = ac_h * c_state + cc_h * kv
        n_ref[h:h + 1, :] = (ac_h[:, 0:QK_DIM] * n_prev[h:h + 1, :]
                             + cc_h[:, 0:QK_DIM] * jnp.sum(kw, axis=0, keepdims=True))
        gate = p32_ref[:, P_MO + h * V_DIM:P_MO + (h + 1) * V_DIM]
        out = _head_norm(hid) * mng_ref[:, vs] * _sigmoid(gate)
        oml_ref[:, vs] = out.astype(BF16)
    y_ml = _dot(oml_ref[...], wpm_ref[:, 0:d_model])

    while side_work:
        side()
    gate_a, gate_b = merge_gates
    mix = _dot((gate_a * y_ret + gate_b * y_ml).astype(BF16), wo_ref[:, 0:d_model])
    return _layer_norm(alpha * x + mix, lng_ref[...], lnb_ref[...])


def _mixer_kernel(x_ref, xn_ref, cos_ref, sin_ref, wmain_ref, wgab_ref, wif_ref, bif_ref,
                  bmerge_ref, convw_ref, convb_ref, rng_ref, mng_ref, wpr_ref, wpm_ref, wo_ref,
                  lng_ref, lnb_ref, dmask_ref, qdec_ref, kdec_ref,
                  o_ref,
                  r_ref, c_ref, n_ref, m_ref, conv_ref, oret_ref, oml_ref,
                  pa32_ref, pa16_ref, paif_ref, pb32_ref, pb16_ref, pbif_ref,
                  *, tm, d_model, alpha, chunk_decay):
    step_id = pl.program_id(1)
    pieces = functools.partial(_project_pieces, wmain_ref=wmain_ref, wif_ref=wif_ref,
                               bif_ref=bif_ref)
    buf_a = dict(p32_ref=pa32_ref, p16_ref=pa16_ref, pif_ref=paif_ref)
    buf_b = dict(p32_ref=pb32_ref, p16_ref=pb16_ref, pif_ref=pbif_ref)
    mix = functools.partial(
        _stage_mix, wgab_ref=wgab_ref, bmerge_ref=bmerge_ref, convw_ref=convw_ref,
        convb_ref=convb_ref, rng_ref=rng_ref, mng_ref=mng_ref, wpr_ref=wpr_ref, wpm_ref=wpm_ref,
        wo_ref=wo_ref, lng_ref=lng_ref, lnb_ref=lnb_ref, dmask_ref=dmask_ref, qdec_ref=qdec_ref,
        kdec_ref=kdec_ref, r_ref=r_ref, c_ref=c_ref, n_ref=n_ref, m_ref=m_ref, conv_ref=conv_ref,
        oret_ref=oret_ref, oml_ref=oml_ref, tm=tm, d_model=d_model, alpha=alpha,
        chunk_decay=chunk_decay)

    @pl.when(step_id == 0)
    def _():
        r_ref[...] = jnp.zeros_like(r_ref)
        c_ref[...] = jnp.zeros_like(c_ref)
        n_ref[...] = jnp.zeros_like(n_ref)
        m_ref[...] = jnp.zeros_like(m_ref)
        conv_ref[0:CONV_PAD, :] = jnp.zeros((CONV_PAD, 2 * _QK), F32)
        for piece in pieces(x_ref, **buf_a):
            piece()

    @pl.when(step_id % 2 == 0)
    def _():
        o_ref[0] = mix(x_ref, cos_ref[...], sin_ref[...], **buf_a, side_work=pieces(xn_ref, **buf_b))

    @pl.when(step_id % 2 == 1)
    def _():
        o_ref[0] = mix(x_ref, cos_ref[...], sin_ref[...], **buf_b, side_work=pieces(xn_ref, **buf_a))


def _ffn_kernel(x_ref, wgu_ref, wd_ref, lng_ref, lnb_ref, o_ref, *, alpha, d_ff):
    for i in range(x_ref.shape[0] // FFN_SUB):
        rows = slice(i * FFN_SUB, (i + 1) * FFN_SUB)
        x = x_ref[rows, :]
        xb = x.astype(BF16)
        gate = _dot(xb, wgu_ref[:, 0:d_ff])
        up = _dot(xb, wgu_ref[:, d_ff:2 * d_ff])
        hidden = (gate * _sigmoid(gate) * up).astype(BF16)
        y = alpha * x + _dot(hidden, wd_ref[...])
        o_ref[rows, :] = _layer_norm(y, lng_ref[...], lnb_ref[...])


def _layer_spec(arr, layer, cols=None):
    tail = arr.shape[1:] if cols is None else arr.shape[1:-1] + (cols,)
    zeros = (0,) * len(tail)
    return pl.BlockSpec((None,) + tail, lambda *_: (layer,) + zeros, pipeline_mode=pl.Buffered(1))


def _const_spec(arr):
    zeros = (0,) * arr.ndim
    return pl.BlockSpec(arr.shape, lambda *_: zeros, pipeline_mode=pl.Buffered(1))


def _mixer_call(x, cos, sin, layer_params, tables, *, layer, alpha, chunk_decay):
    batch, seq, d_model = x.shape
    tm = MIXER_TILE
    assert seq % tm == 0
    n_tiles = seq // tm
    kern = functools.partial(_mixer_kernel, tm=tm, d_model=d_model, alpha=alpha,
                             chunk_decay=chunk_decay)
    handoff = [pltpu.VMEM((tm, P_WIDTH), F32),
               pltpu.VMEM((tm, 2 * _V), BF16),
               pltpu.VMEM((GATE_ROWS, tm), F32)]
    return pl.pallas_call(
        kern,
        grid=(batch, n_tiles),
        in_specs=[pl.BlockSpec((1, tm, d_model), lambda b, j: (b, j, 0)),
                  pl.BlockSpec((1, tm, d_model),
                               lambda b, j: (b, jnp.minimum(j + 1, n_tiles - 1), 0)),
                  pl.BlockSpec((tm, QK_DIM), lambda b, j: (j, 0)),
                  pl.BlockSpec((tm, QK_DIM), lambda b, j: (j, 0))]
                 + [_layer_spec(layer_params[0], layer, cols=OFF_IF + LANE_PAD)]
                 + [_layer_spec(p, layer) for p in layer_params[1:]]
                 + [_const_spec(t) for t in tables],
        out_specs=pl.BlockSpec((1, tm, d_model), lambda b, j: (b, j, 0)),
        out_shape=jax.ShapeDtypeStruct(x.shape, x.dtype),
        scratch_shapes=[
            pltpu.VMEM((HEADS, QK_DIM, V_DIM), F32),
            pltpu.VMEM((HEADS, QK_DIM, V_DIM), F32),
            pltpu.VMEM((8, QK_DIM), F32),
            pltpu.VMEM((8, 128), F32),
            pltpu.VMEM((CONV_PAD + tm, 2 * _QK), F32),
            pltpu.VMEM((tm, _V), BF16),
            pltpu.VMEM((tm, _V), BF16),
        ] + handoff + handoff,
        compiler_params=pltpu.CompilerParams(
            dimension_semantics=("arbitrary", "arbitrary"), vmem_limit_bytes=VMEM_LIMIT),
        name="mixer",
    )(x, x, cos, sin, *layer_params, *tables)


def _ffn_call(x2d, layer_params, *, layer, alpha):
    tokens, d_model = x2d.shape
    tm = min(FFN_TILE, tokens)
    assert tokens % tm == 0 and tm % FFN_SUB == 0
    d_ff = layer_params[1].shape[1]
    return pl.pallas_call(
        functools.partial(_ffn_kernel, alpha=alpha, d_ff=d_ff),
        grid=(tokens // tm,),
        in_specs=[pl.BlockSpec((tm, d_model), lambda i: (i, 0))]
                 + [_layer_spec(p, layer) for p in layer_params],
        out_specs=pl.BlockSpec((tm, d_model), lambda i: (i, 0)),
        out_shape=jax.ShapeDtypeStruct(x2d.shape, x2d.dtype),
        compiler_params=pltpu.CompilerParams(
            dimension_semantics=("arbitrary",), vmem_limit_bytes=VMEM_LIMIT),
        name="ffn",
    )(x2d, *layer_params)


def _position_tables(seq):
    half = QK_DIM // 2
    inv_freq = ROPE_BASE ** (-jnp.arange(half, dtype=F32) / half)
    ang = jnp.arange(seq, dtype=jnp.int32).astype(F32)[:, None] * inv_freq[None, :]
    cos = jnp.cos(ang)
    sin = jnp.sin(ang)
    return jnp.concatenate([cos, cos], axis=-1), jnp.concatenate([-sin, sin], axis=-1)


def _decay_tables(chunk):
    scale = QK_DIM ** -0.5
    log_gamma = jnp.log(1.0 - jnp.power(2.0, -5.0 - jnp.arange(HEADS, dtype=F32)))
    idx = jnp.arange(chunk, dtype=F32)
    rel = idx[:, None] - idx[None, :]
    causal = rel >= 0
    dmask = jnp.where(causal[None],
                      jnp.exp(log_gamma[:, None, None] * jnp.where(causal, rel, 0.0)[None]), 0.0) * scale
    qdec = jnp.exp(log_gamma[:, None] * (idx + 1.0)[None, :])
    kdec = jnp.exp(log_gamma[:, None] * (chunk - 1 - idx)[None, :]) * scale
    qdec = jnp.broadcast_to(qdec[:, :, None], (HEADS, chunk, QK_DIM))
    kdec = jnp.broadcast_to(kdec[:, None, :], (HEADS, QK_DIM, chunk))
    log_gamma64 = np.log(1.0 - np.power(2.0, -5.0 - np.arange(HEADS, dtype=np.float64)))
    chunk_decay = tuple(float(v) for v in np.exp(log_gamma64 * chunk))
    return dmask, qdec, kdec, chunk_decay


def kernel(x, w_in, b_if, b_merge, conv_w, conv_b, ret_norm_g, mlstm_norm_g, w_proj_ret,
           w_proj_mlstm, w_out, ln1_g, ln1_b, w_gate_up, w_down, ln2_g, ln2_b):
    batch, seq, d_model = x.shape
    depth = w_in.shape[0]
    d_ff = w_down.shape[1]
    alpha = float((2 * depth) ** 0.25)
    assert seq % MIXER_TILE == 0 and d_model % 128 == 0

    with jax.ensure_compile_time_eval():
        cos, sin = _position_tables(seq)
        dmask, qdec, kdec, chunk_decay = _decay_tables(MIXER_TILE)

    def rows(v):
        return v.reshape(depth, 1, -1).astype(F32)

    def weight(w):
        return jnp.pad(w.astype(BF16), ((0, 0), (0, 0), (0, LANE_PAD)))

    off_i = OFF_IF
    w_main = w_in.astype(BF16)
    w_gab = weight(w_in[:, :, off_i + 2 * HEADS:])
    w_if = w_in[:, :, off_i:off_i + 2 * HEADS]
    zcol = jnp.zeros((depth, d_model, 8 - HEADS), w_if.dtype)
    ztail = jnp.zeros((depth, d_model, GATE_COLS - GATE_ROWS), w_if.dtype)
    w_if = jnp.concatenate([w_if[:, :, :HEADS], zcol, w_if[:, :, HEADS:], zcol, ztail], axis=-1).astype(BF16)
    zb = jnp.zeros((depth, 8 - HEADS), b_if.dtype)
    zbt = jnp.zeros((depth, GATE_COLS - GATE_ROWS), b_if.dtype)
    bif = jnp.concatenate([b_if[:, :HEADS], zb, b_if[:, HEADS:], zb, zbt], axis=1).astype(F32)
    bif = bif[:, None, :]

    mixer_params = (w_main, w_gab, w_if, bif, rows(b_merge), conv_w.astype(F32), rows(conv_b),
                    rows(ret_norm_g), rows(mlstm_norm_g), weight(w_proj_ret),
                    weight(w_proj_mlstm), weight(w_out), rows(ln1_g), rows(ln1_b))
    ffn_params = (w_gate_up.astype(BF16), w_down.astype(BF16), rows(ln2_g), rows(ln2_b))
    assert w_gate_up.shape[2] == 2 * d_ff and d_ff % 128 == 0

    for l in range(depth):
        x = _mixer_call(x, cos, sin, mixer_params, (dmask, qdec, kdec), layer=l, alpha=alpha,
                        chunk_decay=chunk_decay)
        x2 = _ffn_call(x.reshape(batch * seq, d_model), ffn_params, layer=l, alpha=alpha)
        x = x2.reshape(batch, seq, d_model)
    return x
```
